```python
import math, functools
import jax, jax.numpy as jnp
from jax import lax
import numpy as np

D_MODEL = 1024
BATCH = 8
SEQ = 2048
DEPTH = 4
DEC_BATCH = 8
DEC_SEQ = 64
PAST_LEN = 4096

CHUNK = 64
QBLOCK = 128
N_HEADS = 8
HEAD_DIM = 64
ATTN_W = N_HEADS * HEAD_DIM
CONV_CH = 512
CONV_W = 31
D_FF = 2816
FFN_CONV_W = 3
N_IN = 3 * ATTN_W + 2 * CONV_CH + 2 * D_MODEL
SPLITS = (ATTN_W, 2 * ATTN_W, 3 * ATTN_W, 3 * ATTN_W + CONV_CH,
          3 * ATTN_W + 2 * CONV_CH, 3 * ATTN_W + 2 * CONV_CH + D_MODEL)
EPS = 1e-6

kernel_name = "hybrid_stickbreak_conformer_convffn_stream_step"


def rmsnorm(x, g):
    xf = x.astype(jnp.float32)
    y = xf * lax.rsqrt(jnp.mean(xf * xf, axis=-1, keepdims=True) + EPS)
    return (y * g.astype(jnp.float32)).astype(x.dtype)


def layernorm(x, g, b):
    xf = x.astype(jnp.float32)
    mu = jnp.mean(xf, axis=-1, keepdims=True)
    xc = xf - mu
    y = xc * lax.rsqrt(jnp.mean(xc * xc, axis=-1, keepdims=True) + EPS)
    return (y * g.astype(jnp.float32) + b.astype(jnp.float32)).astype(x.dtype)


def causal_dwconv(x, buf, w, b):
    width, ch = w.shape
    xp = jnp.concatenate([buf.astype(x.dtype), x], axis=1)
    y = lax.conv_general_dilated(xp, w[:, None, :].astype(x.dtype), window_strides=(1,),
                                 padding='VALID', dimension_numbers=('NWC', 'WIO', 'NWC'),
                                 feature_group_count=ch)
    return y + b, xp[:, -(width - 1):]


def sb_attend(q, k, v, q_pos, k_pos):
    z = jnp.einsum('bqhd,bkhd->bhqk', q.astype(jnp.float32), k.astype(jnp.float32)) * (HEAD_DIM ** -0.5)
    mask = k_pos[None, :] < q_pos[:, None]
    log_one_minus = jnp.where(mask, jax.nn.log_sigmoid(-z), 0.0)
    between = lax.cumsum(log_one_minus, axis=3, reverse=True) - log_one_minus
    weights = jnp.where(mask, jnp.exp(jax.nn.log_sigmoid(z) + between), 0.0)
    return jnp.einsum('bhqk,bkhd->bqhd', weights, v.astype(jnp.float32))


def sb_prompt(q, k, v):
    b, s, h, d = q.shape
    nb = s // QBLOCK
    qb = q.reshape(b, nb, QBLOCK, h, d).transpose(1, 0, 2, 3, 4)
    qpos = jnp.arange(s).reshape(nb, QBLOCK)
    kpos = jnp.arange(s)
    out = lax.map(lambda a: sb_attend(a[0], k, v, a[1], kpos), (qb, qpos))
    return out.transpose(1, 0, 2, 3, 4).reshape(b, s, h, d)


def sb_sample(q, k, v, k_past, v_past):
    p = k_past.shape[1]
    t = q.shape[1]
    k_all = jnp.concatenate([k_past.astype(k.dtype), k], axis=1)
    v_all = jnp.concatenate([v_past.astype(v.dtype), v], axis=1)
    return sb_attend(q, k_all, v_all, p + jnp.arange(t), jnp.arange(p + t))


def layer(x, conv_buf, ffn_buf, attend, norm1_g, w_in, b_in, q_norm_g, k_norm_g, w_attn_out,
          dw_w, dw_b, cn_g, cn_b, w_conv_out, b_conv_out, w_out, norm2_g, w_up,
          ffn_dw_w, ffn_dw_b, w_down):
    b, t, _ = x.shape
    h = rmsnorm(x, norm1_g)
    proj = h @ w_in + b_in
    q, k, v, glu_a, glu_b, g_attn, g_conv = jnp.split(proj, SPLITS, axis=-1)
    q = rmsnorm(q.reshape(b, t, N_HEADS, HEAD_DIM), q_norm_g)
    k = rmsnorm(k.reshape(b, t, N_HEADS, HEAD_DIM), k_norm_g)
    v = v.reshape(b, t, N_HEADS, HEAD_DIM)
    a = attend(q, k, v).astype(x.dtype).reshape(b, t, ATTN_W) @ w_attn_out
    u = glu_a * jax.nn.sigmoid(glu_b)
    c, new_conv = causal_dwconv(u, conv_buf, dw_w, dw_b)
    c = jax.nn.silu(layernorm(c, cn_g, cn_b)) @ w_conv_out + b_conv_out
    x = x + (jax.nn.sigmoid(g_attn) * a + jax.nn.sigmoid(g_conv) * c) @ w_out
    up, new_ffn = causal_dwconv(rmsnorm(x, norm2_g) @ w_up, ffn_buf, ffn_dw_w, ffn_dw_b)
    gate, val = jnp.split(up, 2, axis=-1)
    x = x + (jax.nn.silu(gate) * val) @ w_down
    return x, k, v, new_conv, new_ffn


def setup_inputs(seed: int = 0) -> dict:
    key = jax.random.key(seed)
    ks = jax.random.split(key, 32)
    f32 = jnp.float32
    nrm = lambda k, shape, s: jax.random.normal(k, shape, f32) * s
    gain = lambda k, shape: 1.0 + 0.02 * jax.random.normal(k, shape, f32)
    return {
        "x_prompt": nrm(ks[0], (BATCH, SEQ, D_MODEL), 1.0),
        "x_sample": nrm(ks[1], (DEC_BATCH, DEC_SEQ, D_MODEL), 1.0),
        "cache_sb_k": nrm(ks[2], (DEPTH, DEC_BATCH, PAST_LEN, N_HEADS, HEAD_DIM), 1.0),
        "cache_sb_v": nrm(ks[3], (DEPTH, DEC_BATCH, PAST_LEN, N_HEADS, HEAD_DIM), 1.0),
        "state_conv": nrm(ks[4], (DEPTH, DEC_BATCH, CONV_W - 1, CONV_CH), 0.5),
        "state_ffn_conv": nrm(ks[5], (DEPTH, DEC_BATCH, FFN_CONV_W - 1, 2 * D_FF), 1.0),
        "norm1_g": gain(ks[6], (DEPTH, D_MODEL)),
        "w_in": nrm(ks[7], (DEPTH, D_MODEL, N_IN), D_MODEL ** -0.5),
        "b_in": nrm(ks[8], (DEPTH, N_IN), 0.02),
        "q_norm_g": gain(ks[9], (DEPTH, HEAD_DIM)),
        "k_norm_g": gain(ks[10], (DEPTH, HEAD_DIM)),
        "w_attn_out": nrm(ks[11], (DEPTH, ATTN_W, D_MODEL), ATTN_W ** -0.5),
        "dw_w": nrm(ks[12], (DEPTH, CONV_W, CONV_CH), CONV_W ** -0.5),
        "dw_b": nrm(ks[13], (DEPTH, CONV_CH), 0.02),
        "cn_g": gain(ks[14], (DEPTH, CONV_CH)),
        "cn_b": nrm(ks[15], (DEPTH, CONV_CH), 0.02),
        "w_conv_out": nrm(ks[16], (DEPTH, CONV_CH, D_MODEL), CONV_CH ** -0.5),
        "b_conv_out": nrm(ks[17], (DEPTH, D_MODEL), 0.02),
        "w_out": nrm(ks[18], (DEPTH, D_MODEL, D_MODEL), D_MODEL ** -0.5),
        "norm2_g": gain(ks[19], (DEPTH, D_MODEL)),
        "w_up": nrm(ks[20], (DEPTH, D_MODEL, 2 * D_FF), D_MODEL ** -0.5),
        "ffn_dw_w": nrm(ks[21], (DEPTH, FFN_CONV_W, 2 * D_FF), FFN_CONV_W ** -0.5),
        "ffn_dw_b": nrm(ks[22], (DEPTH, 2 * D_FF), 0.02),
        "w_down": nrm(ks[23], (DEPTH, D_FF, D_MODEL), D_FF ** -0.5),
        "final_g": gain(ks[24], (D_MODEL,)),
    }


def reference(x_prompt, x_sample, cache_sb_k, cache_sb_v, state_conv, state_ffn_conv,
              norm1_g, w_in, b_in, q_norm_g, k_norm_g, w_attn_out, dw_w, dw_b, cn_g, cn_b,
              w_conv_out, b_conv_out, w_out, norm2_g, w_up, ffn_dw_w, ffn_dw_b, w_down, final_g):
    bp = x_prompt.shape[0]
    conv_zero = jnp.zeros((bp, CONV_W - 1, CONV_CH), x_prompt.dtype)
    ffn_zero = jnp.zeros((bp, FFN_CONV_W - 1, 2 * D_FF), x_prompt.dtype)
    xp, xs = x_prompt, x_sample
    kp, vp, ks_, vs_, cp, cs, fp, fs = [], [], [], [], [], [], [], []
    for l in range(DEPTH):
        w = (norm1_g[l], w_in[l], b_in[l], q_norm_g[l], k_norm_g[l], w_attn_out[l],
             dw_w[l], dw_b[l], cn_g[l], cn_b[l], w_conv_out[l], b_conv_out[l], w_out[l],
             norm2_g[l], w_up[l], ffn_dw_w[l], ffn_dw_b[l], w_down[l])
        xp, k1, v1, c1, f1 = layer(xp, conv_zero, ffn_zero, sb_prompt, *w)
        attend_s = functools.partial(sb_sample, k_past=cache_sb_k[l], v_past=cache_sb_v[l])
        xs, k2, v2, c2, f2 = layer(xs, state_conv[l], state_ffn_conv[l], attend_s, *w)
        kp.append(k1); vp.append(v1); cp.append(c1); fp.append(f1)
        ks_.append(k2); vs_.append(v2); cs.append(c2); fs.append(f2)
    y_prompt = rmsnorm(xp, final_g)
    y_sample = rmsnorm(xs, final_g)
    return (y_prompt, y_sample, jnp.stack(kp), jnp.stack(vp), jnp.stack(ks_), jnp.stack(vs_),
            jnp.stack(cp), jnp.stack(cs), jnp.stack(fp), jnp.stack(fs))
```

```python
import functools

import numpy as np
import jax
import jax.numpy as jnp
from jax import lax
from jax.experimental import pallas as pl
from jax.experimental.pallas import tpu as pltpu

D_MODEL = 1024
N_HEADS = 8
HEAD_DIM = 64
ATTN_W = N_HEADS * HEAD_DIM
CONV_CH = 512
CONV_W = 31
D_FF = 2816
FFN_CONV_W = 3
N_IN = 3 * ATTN_W + 2 * CONV_CH + 2 * D_MODEL
EPS = 1e-6

LANES = 128
SUBLANES = 8
KEY_BLOCK = 128
HEAD_PAIRS = ATTN_W // LANES
CONV_HALO = 32
FFN_HALO = SUBLANES
FFN_CHUNK = 256
VMEM_LIMIT = 56 * 1024 * 1024

F32 = jnp.float32
BF16 = jnp.bfloat16


def _const_spec(shape):
    return pl.BlockSpec(shape, lambda *_: (0,) * len(shape), pipeline_mode=pl.Buffered(1))


def _sigmoid(x):
    return 1.0 / (1.0 + jnp.exp(-x))


def _inproj_kernel(x_ref, g1_ref, win_ref, bin_ref, qg_ref, kg_ref, hm_ref,
                   q_ref, k_ref, v_ref, kb_ref, vb_ref, u_ref, sga_ref, sgc_ref):
    bb, tm, _ = x_ref.shape
    rows = bb * tm
    x = x_ref[...].reshape(rows, D_MODEL)
    h = x * lax.rsqrt(jnp.mean(x * x, axis=-1, keepdims=True) + EPS) * g1_ref[...]
    h = h.astype(BF16)

    def proj(c0, width):
        return (jnp.dot(h, win_ref[:, c0:c0 + width], preferred_element_type=F32)
                + bin_ref[:, c0:c0 + width])

    def head_rmsnorm(y, g):
        sq = y * y
        hi = sq.astype(BF16)
        lo = (sq - hi.astype(F32)).astype(BF16)
        ssum = (jnp.dot(hi, hm_ref[...], preferred_element_type=F32)
                + jnp.dot(lo, hm_ref[...], preferred_element_type=F32))
        return y * lax.rsqrt(ssum * (1.0 / HEAD_DIM) + EPS) * g

    q = head_rmsnorm(proj(0, ATTN_W), qg_ref[...])
    q_ref[...] = (q * (HEAD_DIM ** -0.5)).astype(BF16).reshape(bb, tm, ATTN_W)
    k = head_rmsnorm(proj(ATTN_W, ATTN_W), kg_ref[...])
    k_ref[...] = k.reshape(bb, tm, ATTN_W)
    kb_ref[...] = k.astype(BF16).reshape(bb, tm, ATTN_W)
    v = proj(2 * ATTN_W, ATTN_W)
    v_ref[...] = v.reshape(bb, tm, ATTN_W)
    vb_ref[...] = v.astype(BF16).reshape(bb, tm, ATTN_W)
    c0 = 3 * ATTN_W
    u = proj(c0, CONV_CH) * _sigmoid(proj(c0 + CONV_CH, CONV_CH))
    u_ref[...] = u.reshape(bb, tm, CONV_CH)
    c0 += 2 * CONV_CH
    sga_ref[...] = _sigmoid(proj(c0, D_MODEL)).astype(BF16).reshape(bb, tm, D_MODEL)
    sgc_ref[...] = _sigmoid(proj(c0 + D_MODEL, D_MODEL)).astype(BF16).reshape(bb, tm, D_MODEL)


def _inproj(x, g1, win, b_in, qg, kg, hm, *, bb, tm):
    nb, t, _ = x.shape
    grid = (nb // bb, t // tm)
    row = lambda width: pl.BlockSpec((bb, tm, width), lambda b, i: (b, i, 0))
    shp = lambda width, dt: jax.ShapeDtypeStruct((nb, t, width), dt)
    return pl.pallas_call(
        _inproj_kernel,
        grid=grid,
        in_specs=[row(D_MODEL), _const_spec((1, D_MODEL)), _const_spec((D_MODEL, N_IN)),
                  _const_spec((1, N_IN)), _const_spec((1, ATTN_W)), _const_spec((1, ATTN_W)),
                  _const_spec((ATTN_W, ATTN_W))],
        out_specs=[row(ATTN_W), row(ATTN_W), row(ATTN_W), row(ATTN_W), row(ATTN_W),
                   row(CONV_CH), row(D_MODEL), row(D_MODEL)],
        out_shape=[shp(ATTN_W, BF16), shp(ATTN_W, F32), shp(ATTN_W, F32), shp(ATTN_W, BF16),
                   shp(ATTN_W, BF16), shp(CONV_CH, F32), shp(D_MODEL, BF16), shp(D_MODEL, BF16)],
        compiler_params=pltpu.CompilerParams(
            dimension_semantics=("arbitrary", "arbitrary"), vmem_limit_bytes=VMEM_LIMIT),
        name="inproj",
    )(x, g1, win, b_in, qg, kg, hm)


def _attn_kernel(*refs, tq, n_masked, n_past):
    if n_past:
        q_ref, kn_ref, vn_ref, kp_ref, vp_ref, um_ref, o_ref = refs
    else:
        q_ref, kn_ref, vn_ref, um_ref, o_ref = refs
    i = pl.program_id(2)
    blocks_per_tile = tq // KEY_BLOCK
    lane = lax.broadcasted_iota(jnp.int32, (KEY_BLOCK, LANES), 1)
    even_head = lane < HEAD_DIM
    q2 = q_ref[0]

    def step(kblk, vblk, carry, acc, mask):
        zero = jnp.zeros_like(kblk)
        kbd = jnp.concatenate([jnp.where(even_head, kblk, zero), jnp.where(even_head, zero, kblk)], axis=0)
        vbd = jnp.concatenate([jnp.where(even_head, vblk, zero), jnp.where(even_head, zero, vblk)], axis=0)
        z = lax.dot_general(q2, kbd, (((1,), (1,)), ((), ())), preferred_element_type=F32)
        sp = jnp.maximum(z, 0.0) + jnp.log(1.0 + jnp.exp(-jnp.abs(z)))
        if mask is not None:
            sp = jnp.where(mask, sp, 0.0)
        sums = jnp.dot(sp.astype(BF16), um_ref[...], preferred_element_type=F32)
        later = sums[:, :2 * KEY_BLOCK]
        total = sums[:, 2 * KEY_BLOCK:]
        w = jnp.exp((z - sp) - later - carry)
        if mask is not None:
            w = jnp.where(mask, w, 0.0)
        acc = acc + jnp.dot(w.astype(BF16), vbd, preferred_element_type=F32)
        return carry + total, acc

    carry = jnp.zeros((tq, 2 * KEY_BLOCK), F32)
    acc = jnp.zeros((tq, LANES), F32)
    row = lax.broadcasted_iota(jnp.int32, (tq, 2 * KEY_BLOCK), 0)
    col = lax.broadcasted_iota(jnp.int32, (tq, 2 * KEY_BLOCK), 1) & (KEY_BLOCK - 1)
    qpos = i * tq + row
    for d in range(n_masked):
        jb = i * blocks_per_tile + (n_masked - 1 - d)
        start = pl.multiple_of(jb * KEY_BLOCK, KEY_BLOCK)
        mask = (jb * KEY_BLOCK + col) < qpos
        carry, acc = step(kn_ref[0, pl.ds(start, KEY_BLOCK), :], vn_ref[0, pl.ds(start, KEY_BLOCK), :],
                          carry, acc, mask)

    n_open = i * blocks_per_tile

    def open_body(n, state):
        start = pl.multiple_of((n_open - 1 - n) * KEY_BLOCK, KEY_BLOCK)
        return step(kn_ref[0, pl.ds(start, KEY_BLOCK), :], vn_ref[0, pl.ds(start, KEY_BLOCK), :],
                    state[0], state[1], None)

    carry, acc = lax.fori_loop(0, n_open, open_body, (carry, acc))

    if n_past:
        def past_body(n, state):
            start = pl.multiple_of((n_past - 1 - n) * KEY_BLOCK, KEY_BLOCK)
            return step(kp_ref[0, pl.ds(start, KEY_BLOCK), :], vp_ref[0, pl.ds(start, KEY_BLOCK), :],
                        state[0], state[1], None)

        carry, acc = lax.fori_loop(0, n_past, past_body, (carry, acc))

    o_ref[0] = acc.astype(BF16)


def _attention(q, k_new, v_new, k_past, v_past, um, *, tq):
    nb, t, _ = q.shape
    tn = k_new.shape[1]
    n_masked = max(1, tq // KEY_BLOCK)
    n_past = 0 if k_past is None else k_past.shape[1] // KEY_BLOCK
    grid = (nb, HEAD_PAIRS, t // tq)
    q_spec = pl.BlockSpec((1, tq, LANES), lambda b, p, i: (b, i, p))
    kv_spec = lambda length: pl.BlockSpec((1, length, LANES), lambda b, p, i: (b, 0, p))
    in_specs = [q_spec, kv_spec(tn), kv_spec(tn)]
    args = [q, k_new, v_new]
    if n_past:
        in_specs += [kv_spec(k_past.shape[1])] * 2
        args += [k_past, v_past]
    in_specs.append(_const_spec(um.shape))
    args.append(um)
    return pl.pallas_call(
        functools.partial(_attn_kernel, tq=tq, n_masked=n_masked, n_past=n_past),
        grid=grid,
        in_specs=in_specs,
        out_specs=q_spec,
        out_shape=jax.ShapeDtypeStruct((nb, t, ATTN_W), BF16),
        compiler_params=pltpu.CompilerParams(
            dimension_semantics=("arbitrary",) * 3, vmem_limit_bytes=VMEM_LIMIT),
        name="attention",
    )(*args)


def _mix_kernel(x_ref, a_ref, u_ref, uh_ref, st_ref, sga_ref, sgc_ref, dww_ref, dwb_ref, cng_ref,
                cnb_ref, wco_ref, bco_ref, wao_ref, wo_ref, x1_ref, ext_ref, act_ref):
    i = pl.program_id(1)
    bb, tm, _ = x_ref.shape
    rows = bb * tm

    @pl.when(i == 0)
    def _():
        ext_ref[:, 0:CONV_HALO, :] = st_ref[...]

    @pl.when(i > 0)
    def _():
        ext_ref[:, 0:CONV_HALO, :] = uh_ref[...]

    ext_ref[:, CONV_HALO:CONV_HALO + tm, :] = u_ref[...]

    chunk = 32
    first = CONV_HALO - (CONV_W - 1)
    for b in range(bb):
        for r0 in range(0, tm, chunk):
            acc = jnp.broadcast_to(dwb_ref[...], (chunk, CONV_CH))
            for j in range(CONV_W):
                acc = acc + ext_ref[b, r0 + first + j:r0 + first + j + chunk, :] * dww_ref[j:j + 1, :]
            mu = jnp.mean(acc, axis=-1, keepdims=True)
            xc = acc - mu
            y = xc * lax.rsqrt(jnp.mean(xc * xc, axis=-1, keepdims=True) + EPS)
            y = y * cng_ref[...] + cnb_ref[...]
            act_ref[b * tm + r0:b * tm + r0 + chunk, :] = (y * _sigmoid(y)).astype(BF16)

    c = jnp.dot(act_ref[...], wco_ref[...], preferred_element_type=F32) + bco_ref[...]
    a = jnp.dot(a_ref[...].reshape(rows, ATTN_W), wao_ref[...], preferred_element_type=F32)
    m = (sga_ref[...].reshape(rows, D_MODEL).astype(F32) * a
         + sgc_ref[...].reshape(rows, D_MODEL).astype(F32) * c)
    x1 = x_ref[...].reshape(rows, D_MODEL) + jnp.dot(m.astype(BF16), wo_ref[...],
                                                     preferred_element_type=F32)
    x1_ref[...] = x1.reshape(bb, tm, D_MODEL)


def _mix(x, a, u, state, sga, sgc, dww, dwb, cng, cnb, wco, bco, wao, wo, *, bb, tm):
    nb, t, _ = x.shape
    grid = (nb // bb, t // tm)
    row = lambda width: pl.BlockSpec((bb, tm, width), lambda b, i: (b, i, 0))
    halo_blocks = tm // CONV_HALO
    halo = pl.BlockSpec((bb, CONV_HALO, CONV_CH),
                        lambda b, i: (b, jnp.maximum(i * halo_blocks - 1, 0), 0))
    st_spec = pl.BlockSpec((bb, CONV_HALO, CONV_CH), lambda b, i: (b, 0, 0))
    return pl.pallas_call(
        _mix_kernel,
        grid=grid,
        in_specs=[row(D_MODEL), row(ATTN_W), row(CONV_CH), halo, st_spec, row(D_MODEL), row(D_MODEL),
                  _const_spec((CONV_W, CONV_CH)), _const_spec((1, CONV_CH)), _const_spec((1, CONV_CH)),
                  _const_spec((1, CONV_CH)), _const_spec((CONV_CH, D_MODEL)), _const_spec((1, D_MODEL)),
                  _const_spec((ATTN_W, D_MODEL)), _const_spec((D_MODEL, D_MODEL))],
        out_specs=row(D_MODEL),
        out_shape=jax.ShapeDtypeStruct((nb, t, D_MODEL), F32),
        scratch_shapes=[pltpu.VMEM((bb, CONV_HALO + tm, CONV_CH), F32),
                        pltpu.VMEM((bb * tm, CONV_CH), BF16)],
        compiler_params=pltpu.CompilerParams(
            dimension_semantics=("arbitrary", "arbitrary"), vmem_limit_bytes=VMEM_LIMIT),
        name="mix",
    )(x, a, u, u, state, sga, sgc, dww, dwb, cng, cnb, wco, bco, wao, wo)


def _ffn_kernel(x1_ref, st_ref, g2_ref, wup_ref, fdw_ref, fdb_ref, wdn_ref, fg_ref,
                y_ref, nf_ref, prev_ref, extg_ref, extv_ref, act_ref, *, final_norm):
    i = pl.program_id(1)
    bb, tm, _ = x1_ref.shape
    rows = bb * tm

    @pl.when(i == 0)
    def _():
        prev_ref[...] = st_ref[...]

    x1 = x1_ref[...].reshape(rows, D_MODEL)
    h = x1 * lax.rsqrt(jnp.mean(x1 * x1, axis=-1, keepdims=True) + EPS) * g2_ref[...]
    h = h.astype(BF16)

    def conv(ext_ref, c0):
        up = jnp.dot(h, wup_ref[:, c0:c0 + FFN_CHUNK], preferred_element_type=F32)
        up = up.reshape(bb, tm, FFN_CHUNK)
        ext_ref[:, 0:FFN_HALO, :] = prev_ref[:, :, c0:c0 + FFN_CHUNK]
        ext_ref[:, FFN_HALO:FFN_HALO + tm, :] = up
        tail = up[:, tm - FFN_HALO:tm, :]
        prev_ref[:, :, c0:c0 + FFN_CHUNK] = tail
        nf_ref[:, :, c0:c0 + FFN_CHUNK] = tail
        w = fdw_ref[:, c0:c0 + FFN_CHUNK]
        out = (ext_ref[:, FFN_HALO - 2:FFN_HALO - 2 + tm, :] * w[0:1]
               + ext_ref[:, FFN_HALO - 1:FFN_HALO - 1 + tm, :] * w[1:2]
               + up * w[2:3] + fdb_ref[:, c0:c0 + FFN_CHUNK])
        return out.reshape(rows, FFN_CHUNK)

    for c in range(D_FF // FFN_CHUNK):
        gate = conv(extg_ref, c * FFN_CHUNK)
        val = conv(extv_ref, D_FF + c * FFN_CHUNK)
        act_ref[:, c * FFN_CHUNK:(c + 1) * FFN_CHUNK] = (gate * _sigmoid(gate) * val).astype(BF16)

    y = x1 + jnp.dot(act_ref[...], wdn_ref[...], preferred_element_type=F32)
    if final_norm:
        y = y * lax.rsqrt(jnp.mean(y * y, axis=-1, keepdims=True) + EPS) * fg_ref[...]
    y_ref[...] = y.reshape(bb, tm, D_MODEL)


def _ffn(x1, state, g2, wup, fdw, fdb, wdn, fg, *, bb, tm, final_norm):
    nb, t, _ = x1.shape
    grid = (nb // bb, t // tm)
    row = pl.BlockSpec((bb, tm, D_MODEL), lambda b, i: (b, i, 0))
    st_spec = pl.BlockSpec((bb, FFN_HALO, 2 * D_FF), lambda b, i: (b, 0, 0))
    return pl.pallas_call(
        functools.partial(_ffn_kernel, final_norm=final_norm),
        grid=grid,
        in_specs=[row, st_spec, _const_spec((1, D_MODEL)), _const_spec((D_MODEL, 2 * D_FF)),
                  _const_spec((FFN_CONV_W, 2 * D_FF)), _const_spec((1, 2 * D_FF)),
                  _const_spec((D_FF, D_MODEL)), _const_spec((1, D_MODEL))],
        out_specs=[row, st_spec],
        out_shape=[jax.ShapeDtypeStruct((nb, t, D_MODEL), F32),
                   jax.ShapeDtypeStruct((nb, FFN_HALO, 2 * D_FF), F32)],
        scratch_shapes=[pltpu.VMEM((bb, FFN_HALO, 2 * D_FF), F32),
                        pltpu.VMEM((bb, FFN_HALO + tm, FFN_CHUNK), F32),
                        pltpu.VMEM((bb, FFN_HALO + tm, FFN_CHUNK), F32),
                        pltpu.VMEM((bb * tm, D_FF), BF16)],
        compiler_params=pltpu.CompilerParams(
            dimension_semantics=("arbitrary", "arbitrary"), vmem_limit_bytes=VMEM_LIMIT),
        name="ffn",
    )(x1, state, g2, wup, fdw, fdb, wdn, fg)


def _cumsum_matrix():
    j = np.arange(2 * KEY_BLOCK)[:, None]
    s = np.arange(2 * KEY_BLOCK)[None, :]
    same = (j // KEY_BLOCK) == (s // KEY_BLOCK)
    later = same & ((j % KEY_BLOCK) > (s % KEY_BLOCK))
    return np.concatenate([later, same], axis=1).astype(np.float32)


def _head_mean_matrix():
    h = np.arange(ATTN_W) // HEAD_DIM
    return (h[:, None] == h[None, :]).astype(np.float32)


def _pad_rows_front(a, rows):
    return jnp.pad(a, ((0, 0), (rows - a.shape[1], 0), (0, 0)))


def kernel(x_prompt, x_sample, cache_sb_k, cache_sb_v, state_conv, state_ffn_conv, norm1_g, w_in, b_in,
           q_norm_g, k_norm_g, w_attn_out, dw_w, dw_b, cn_g, cn_b, w_conv_out, b_conv_out, w_out,
           norm2_g, w_up, ffn_dw_w, ffn_dw_b, w_down, final_g):
    depth = w_in.shape[0]
    bp, seq, _ = x_prompt.shape
    bs, dec, _ = x_sample.shape
    past = cache_sb_k.shape[2]
    um = jnp.asarray(_cumsum_matrix(), BF16)
    hm = jnp.asarray(_head_mean_matrix(), BF16)
    row2 = lambda a: a.reshape(1, -1)

    past_k = cache_sb_k.reshape(depth, bs, past, ATTN_W).astype(BF16)
    past_v = cache_sb_v.reshape(depth, bs, past, ATTN_W).astype(BF16)
    conv_zero = jnp.zeros((bp, CONV_HALO, CONV_CH), F32)
    ffn_zero = jnp.zeros((bp, FFN_HALO, 2 * D_FF), F32)

    xp, xs = x_prompt, x_sample
    outs = {name: [] for name in ("kp", "vp", "ks", "vs", "cp", "cs", "fp", "fs")}
    for l in range(depth):
        win = w_in[l].astype(BF16)
        wao = w_attn_out[l].astype(BF16)
        wco = w_conv_out[l].astype(BF16)
        wo = w_out[l].astype(BF16)
        wup = w_up[l].astype(BF16)
        wdn = w_down[l].astype(BF16)
        qg = row2(jnp.tile(q_norm_g[l], N_HEADS))
        kg = row2(jnp.tile(k_norm_g[l], N_HEADS))
        final_norm = l == depth - 1

        def run(x, conv_state, ffn_state, k_past, v_past, bb, tm, tq):
            q, k, v, kb, vb, u, sga, sgc = _inproj(x, row2(norm1_g[l]), win, row2(b_in[l]), qg, kg, hm,
                                                   bb=bb, tm=tm)
            t = x.shape[1]
            if t % KEY_BLOCK:
                padding = ((0, 0), (0, KEY_BLOCK - t % KEY_BLOCK), (0, 0))
                kb, vb = jnp.pad(kb, padding), jnp.pad(vb, padding)
            a = _attention(q, kb, vb, k_past, v_past, um, tq=tq)
            x1 = _mix(x, a, u, conv_state, sga, sgc, dw_w[l], row2(dw_b[l]), row2(cn_g[l]), row2(cn_b[l]),
                      wco, row2(b_conv_out[l]), wao, wo, bb=bb, tm=tm)
            y, nf = _ffn(x1, ffn_state, row2(norm2_g[l]), wup, ffn_dw_w[l], row2(ffn_dw_b[l]), wdn,
                         row2(final_g), bb=bb, tm=tm, final_norm=final_norm)
            return y, k, v, u[:, t - (CONV_W - 1):, :], nf[:, FFN_HALO - (FFN_CONV_W - 1):, :]

        xp, k1, v1, c1, f1 = run(xp, conv_zero, ffn_zero, None, None, 1, 256, 256)
        xs, k2, v2, c2, f2 = run(xs, _pad_rows_front(state_conv[l], CONV_HALO),
                                 _pad_rows_front(state_ffn_conv[l], FFN_HALO),
                                 past_k[l], past_v[l], bs, dec, dec)
        for name, val in zip(("kp", "vp", "ks", "vs", "cp", "cs", "fp", "fs"), (k1, v1, k2, v2, c1, c2, f1, f2)):
            outs[name].append(val)

    heads = lambda lst, b, t: jnp.stack(lst).reshape(depth, b, t, N_HEADS, HEAD_DIM)
    return (xp, xs, heads(outs["kp"], bp, seq), heads(outs["vp"], bp, seq),
            heads(outs["ks"], bs, dec), heads(outs["vs"], bs, dec),
            jnp.stack(outs["cp"]), jnp.stack(outs["cs"]), jnp.stack(outs["fp"]), jnp.stack(outs["fs"]))
```

```python
import functools

import numpy as np
import jax
import jax.numpy as jnp
from jax import lax
from jax.experimental import pallas as pl
from jax.experimental.pallas import tpu as pltpu

D_MODEL = 1024
N_HEADS = 8
HEAD_DIM = 64
ATTN_W = N_HEADS * HEAD_DIM
CONV_CH = 512
CONV_W = 31
D_FF = 2816
FFN_CONV_W = 3
N_IN = 3 * ATTN_W + 2 * CONV_CH + 2 * D_MODEL
EPS = 1e-6

LANES = 128
SUBLANES = 8
KEY_BLOCK = 128
HEAD_PAIRS = ATTN_W // LANES
CONV_HALO = 32
FFN_HALO = SUBLANES
FFN_CHUNK = 256
VMEM_LIMIT = 56 * 1024 * 1024
LOG2E = 1.4426950408889634
CARRY_LIMIT = 152.0

F32 = jnp.float32
BF16 = jnp.bfloat16


def _const_spec(shape):
    return pl.BlockSpec(shape, lambda *_: (0,) * len(shape), pipeline_mode=pl.Buffered(1))


def _sigmoid(x):
    return 1.0 / (1.0 + jnp.exp(-x))


def _inproj_kernel(x_ref, g1_ref, win_ref, bin_ref, qg_ref, kg_ref, hm_ref,
                   q_ref, k_ref, v_ref, kbt_ref, vbd_ref, u_ref, sga_ref, sgc_ref):
    bb, tm, _ = x_ref.shape
    rows = bb * tm
    x = x_ref[...].reshape(rows, D_MODEL)
    h = x * lax.rsqrt(jnp.mean(x * x, axis=-1, keepdims=True) + EPS) * g1_ref[...]
    h = h.astype(BF16)

    def proj(c0, width):
        return (jnp.dot(h, win_ref[:, c0:c0 + width], preferred_element_type=F32)
                + bin_ref[:, c0:c0 + width])

    def head_rmsnorm(y, g):
        sq = y * y
        hi = sq.astype(BF16)
        lo = (sq - hi.astype(F32)).astype(BF16)
        ssum = (jnp.dot(hi, hm_ref[...], preferred_element_type=F32)
                + jnp.dot(lo, hm_ref[...], preferred_element_type=F32))
        return y * lax.rsqrt(ssum * (1.0 / HEAD_DIM) + EPS) * g

    q = head_rmsnorm(proj(0, ATTN_W), qg_ref[...])
    q_ref[...] = (q * (HEAD_DIM ** -0.5 * LOG2E)).astype(BF16).reshape(bb, tm, ATTN_W)
    k = head_rmsnorm(proj(ATTN_W, ATTN_W), kg_ref[...])
    k_ref[...] = k.reshape(bb, tm, ATTN_W)
    v = proj(2 * ATTN_W, ATTN_W)
    v_ref[...] = v.reshape(bb, tm, ATTN_W)

    blk = min(tm, KEY_BLOCK)
    even_row = (lax.broadcasted_iota(jnp.int32, (ATTN_W, KEY_BLOCK), 0) & (LANES - 1)) < HEAD_DIM
    even_lane = (lax.broadcasted_iota(jnp.int32, (KEY_BLOCK, ATTN_W), 1) & (LANES - 1)) < HEAD_DIM
    for b in range(bb):
        for c in range(tm // blk):
            r0 = b * tm + c * blk
            kb, vb = k[r0:r0 + blk], v[r0:r0 + blk]
            if blk < KEY_BLOCK:
                fill = jnp.zeros((KEY_BLOCK - blk, ATTN_W), F32)
                kb, vb = jnp.concatenate([kb, fill], axis=0), jnp.concatenate([vb, fill], axis=0)
            kt = kb.T
            o = 2 * c * KEY_BLOCK
            kbt_ref[b, :, o:o + KEY_BLOCK] = jnp.where(even_row, kt, 0.0).astype(BF16)
            kbt_ref[b, :, o + KEY_BLOCK:o + 2 * KEY_BLOCK] = jnp.where(even_row, 0.0, kt).astype(BF16)
            vbd_ref[b, o:o + KEY_BLOCK, :] = jnp.where(even_lane, vb, 0.0).astype(BF16)
            vbd_ref[b, o + KEY_BLOCK:o + 2 * KEY_BLOCK, :] = jnp.where(even_lane, 0.0, vb).astype(BF16)

    c0 = 3 * ATTN_W
    u = proj(c0, CONV_CH) * _sigmoid(proj(c0 + CONV_CH, CONV_CH))
    u_ref[...] = u.reshape(bb, tm, CONV_CH)
    c0 += 2 * CONV_CH
    sga_ref[...] = _sigmoid(proj(c0, D_MODEL)).astype(BF16).reshape(bb, tm, D_MODEL)
    sgc_ref[...] = _sigmoid(proj(c0 + D_MODEL, D_MODEL)).astype(BF16).reshape(bb, tm, D_MODEL)


def _inproj(x, g1, win, b_in, qg, kg, hm, *, bb, tm):
    nb, t, _ = x.shape
    grid = (nb // bb, t // tm)
    row = lambda width: pl.BlockSpec((bb, tm, width), lambda b, i: (b, i, 0))
    shp = lambda width, dt: jax.ShapeDtypeStruct((nb, t, width), dt)
    kw = 2 * KEY_BLOCK * (tm // min(tm, KEY_BLOCK))
    return pl.pallas_call(
        _inproj_kernel,
        grid=grid,
        in_specs=[row(D_MODEL), _const_spec((1, D_MODEL)), _const_spec((D_MODEL, N_IN)),
                  _const_spec((1, N_IN)), _const_spec((1, ATTN_W)), _const_spec((1, ATTN_W)),
                  _const_spec((ATTN_W, ATTN_W))],
        out_specs=[row(ATTN_W), row(ATTN_W), row(ATTN_W),
                   pl.BlockSpec((bb, ATTN_W, kw), lambda b, i: (b, 0, i)),
                   pl.BlockSpec((bb, kw, ATTN_W), lambda b, i: (b, i, 0)),
                   row(CONV_CH), row(D_MODEL), row(D_MODEL)],
        out_shape=[shp(ATTN_W, BF16), shp(ATTN_W, F32), shp(ATTN_W, F32),
                   jax.ShapeDtypeStruct((nb, ATTN_W, kw * (t // tm)), BF16),
                   jax.ShapeDtypeStruct((nb, kw * (t // tm), ATTN_W), BF16),
                   shp(CONV_CH, F32), shp(D_MODEL, BF16), shp(D_MODEL, BF16)],
        compiler_params=pltpu.CompilerParams(
            dimension_semantics=("arbitrary", "arbitrary"), vmem_limit_bytes=VMEM_LIMIT),
        name="inproj",
    )(x, g1, win, b_in, qg, kg, hm)


def _attn_kernel(*refs, tq, n_masked, n_past):
    if n_past:
        q_ref, kn_ref, vn_ref, kp_ref, vp_ref, um_ref, o_ref, carry_ref, acc_ref, sp_ref, ls_ref = refs
    else:
        q_ref, kn_ref, vn_ref, um_ref, o_ref, carry_ref, acc_ref, sp_ref, ls_ref = refs
    i = pl.program_id(1)
    blocks_per_tile = tq // KEY_BLOCK
    carry_ref[...] = jnp.zeros_like(carry_ref)
    acc_ref[...] = jnp.zeros_like(acc_ref)

    def step(scores, values, mask):
        for p in range(HEAD_PAIRS):
            z = scores(p)
            sp = jnp.where(z > 64.0, z, jnp.log(1.0 + jnp.exp2(z)) * LOG2E)
            if mask is not None:
                sp = jnp.where(mask, sp, 0.0)
            sp_ref[p * tq:(p + 1) * tq, :] = sp.astype(BF16)
            ls_ref[p * tq:(p + 1) * tq, :] = z - sp
        later_all = jnp.dot(sp_ref[...], um_ref[...], preferred_element_type=F32)
        smallest = None
        for p in range(HEAD_PAIRS):
            later = later_all[p * tq:(p + 1) * tq]
            w = jnp.exp2(ls_ref[p * tq:(p + 1) * tq, :] - later - carry_ref[p])
            if mask is not None:
                w = jnp.where(mask, w, 0.0)
            acc_ref[p] += jnp.dot(w.astype(BF16), values(p), preferred_element_type=F32)
            halves = []
            for c0 in (0, KEY_BLOCK):
                own = sp_ref[p * tq:(p + 1) * tq, c0:c0 + KEY_BLOCK].astype(F32)
                halves.append(jnp.broadcast_to(later[:, c0:c0 + 1] + own[:, 0:1], (tq, KEY_BLOCK)))
            carry = carry_ref[p] + jnp.concatenate(halves, axis=1)
            carry_ref[p] = carry
            smallest = carry if smallest is None else jnp.minimum(smallest, carry)
        return jnp.min(smallest)

    def new_block(jb):
        start = pl.multiple_of(jb * 2 * KEY_BLOCK, 2 * KEY_BLOCK)
        scores = lambda p: jnp.dot(q_ref[0, :, p * LANES:(p + 1) * LANES],
                                   kn_ref[0, p * LANES:(p + 1) * LANES, pl.ds(start, 2 * KEY_BLOCK)],
                                   preferred_element_type=F32)
        values = lambda p: vn_ref[0, pl.ds(start, 2 * KEY_BLOCK), p * LANES:(p + 1) * LANES]
        return scores, values

    def past_block(jb):
        start = pl.multiple_of(jb * KEY_BLOCK, KEY_BLOCK)
        even_head = lax.broadcasted_iota(jnp.int32, (KEY_BLOCK, LANES), 1) < HEAD_DIM

        def split(ref, p):
            blk = ref[0, pl.ds(start, KEY_BLOCK), p * LANES:(p + 1) * LANES]
            zero = jnp.zeros_like(blk)
            return jnp.concatenate([jnp.where(even_head, blk, zero), jnp.where(even_head, zero, blk)], axis=0)

        scores = lambda p: lax.dot_general(q_ref[0, :, p * LANES:(p + 1) * LANES], split(kp_ref, p),
                                           (((1,), (1,)), ((), ())), preferred_element_type=F32)
        values = lambda p: split(vp_ref, p)
        return scores, values

    row = lax.broadcasted_iota(jnp.int32, (tq, 2 * KEY_BLOCK), 0)
    col = lax.broadcasted_iota(jnp.int32, (tq, 2 * KEY_BLOCK), 1) & (KEY_BLOCK - 1)
    qpos = i * tq + row
    smallest = jnp.float32(0.0)
    for d in range(n_masked):
        jb = i * blocks_per_tile + (n_masked - 1 - d)
        smallest = step(*new_block(jb), (jb * KEY_BLOCK + col) < qpos)

    def walk(block, n_blocks, smallest):
        def cond(state):
            return jnp.logical_and(state[0] < n_blocks, state[1] < CARRY_LIMIT)

        def body(state):
            return state[0] + 1, step(*block(n_blocks - 1 - state[0]), None)

        return lax.while_loop(cond, body, (jnp.int32(0), smallest))[1]

    smallest = walk(new_block, i * blocks_per_tile, smallest)
    if n_past:
        walk(past_block, n_past, smallest)

    for p in range(HEAD_PAIRS):
        o_ref[0, :, p * LANES:(p + 1) * LANES] = acc_ref[p].astype(BF16)


def _attention(q, kbt_new, vbd_new, k_past, v_past, um, *, tq):
    nb, t, _ = q.shape
    tn2 = vbd_new.shape[1]
    n_masked = max(1, tq // KEY_BLOCK)
    n_past = 0 if k_past is None else k_past.shape[1] // KEY_BLOCK
    grid = (nb, t // tq)
    q_spec = pl.BlockSpec((1, tq, ATTN_W), lambda b, i: (b, i, 0))
    kv_spec = lambda length: pl.BlockSpec((1, length, ATTN_W), lambda b, i: (b, 0, 0))
    in_specs = [q_spec, pl.BlockSpec((1, ATTN_W, tn2), lambda b, i: (b, 0, 0)), kv_spec(tn2)]
    args = [q, kbt_new, vbd_new]
    if n_past:
        in_specs += [kv_spec(k_past.shape[1])] * 2
        args += [k_past, v_past]
    in_specs.append(_const_spec(um.shape))
    args.append(um)
    return pl.pallas_call(
        functools.partial(_attn_kernel, tq=tq, n_masked=n_masked, n_past=n_past),
        grid=grid,
        in_specs=in_specs,
        out_specs=q_spec,
        out_shape=jax.ShapeDtypeStruct((nb, t, ATTN_W), BF16),
        scratch_shapes=[pltpu.VMEM((HEAD_PAIRS, tq, 2 * KEY_BLOCK), F32),
                        pltpu.VMEM((HEAD_PAIRS, tq, LANES), F32),
                        pltpu.VMEM((HEAD_PAIRS * tq, 2 * KEY_BLOCK), BF16),
                        pltpu.VMEM((HEAD_PAIRS * tq, 2 * KEY_BLOCK), F32)],
        compiler_params=pltpu.CompilerParams(
            dimension_semantics=("arbitrary",) * 2, vmem_limit_bytes=VMEM_LIMIT),
        name="attention",
    )(*args)


def _mix_kernel(x_ref, a_ref, u_ref, uh_ref, st_ref, sga_ref, sgc_ref, dww_ref, dwb_ref, cng_ref,
                cnb_ref, wco_ref, bco_ref, wao_ref, wo_ref, x1_ref, ext_ref, act_ref):
    i = pl.program_id(1)
    bb, tm, _ = x_ref.shape
    rows = bb * tm

    @pl.when(i == 0)
    def _():
        ext_ref[0, :, 0:CONV_HALO, :] = st_ref[...]

    @pl.when(i > 0)
    def _():
        ext_ref[0, :, 0:CONV_HALO, :] = uh_ref[...]

    ext_ref[0, :, CONV_HALO:CONV_HALO + tm, :] = u_ref[...]
    span = CONV_HALO + tm - SUBLANES
    for s in range(1, SUBLANES):
        ext_ref[s, :, 0:span, :] = ext_ref[0, :, s:s + span, :]

    chunk = 32
    first = CONV_HALO - (CONV_W - 1)
    for b in range(bb):
        for r0 in range(0, tm, chunk):
            acc = jnp.broadcast_to(dwb_ref[...], (chunk // SUBLANES, SUBLANES, CONV_CH))
            for j in range(CONV_W):
                s, base = (first + j) % SUBLANES, (first + j) // SUBLANES * SUBLANES
                taps = ext_ref[s, b, r0 + base:r0 + base + chunk, :]
                acc = acc + taps.reshape(chunk // SUBLANES, SUBLANES, CONV_CH) * dww_ref[j]
            acc = acc.reshape(chunk, CONV_CH)
            mu = jnp.mean(acc, axis=-1, keepdims=True)
            xc = acc - mu
            y = xc * lax.rsqrt(jnp.mean(xc * xc, axis=-1, keepdims=True) + EPS)
            y = y * cng_ref[...] + cnb_ref[...]
            act_ref[b * tm + r0:b * tm + r0 + chunk, :] = (y * _sigmoid(y)).astype(BF16)

    c = jnp.dot(act_ref[...], wco_ref[...], preferred_element_type=F32) + bco_ref[...]
    a = jnp.dot(a_ref[...].reshape(rows, ATTN_W), wao_ref[...], preferred_element_type=F32)
    m = (sga_ref[...].reshape(rows, D_MODEL).astype(F32) * a
         + sgc_ref[...].reshape(rows, D_MODEL).astype(F32) * c)
    x1 = x_ref[...].reshape(rows, D_MODEL) + jnp.dot(m.astype(BF16), wo_ref[...],
                                                     preferred_element_type=F32)
    x1_ref[...] = x1.reshape(bb, tm, D_MODEL)


def _mix(x, a, u, state, sga, sgc, dww, dwb, cng, cnb, wco, bco, wao, wo, *, bb, tm):
    nb, t, _ = x.shape
    grid = (nb // bb, t // tm)
    row = lambda width: pl.BlockSpec((bb, tm, width), lambda b, i: (b, i, 0))
    halo_blocks = tm // CONV_HALO
    halo = pl.BlockSpec((bb, CONV_HALO, CONV_CH),
                        lambda b, i: (b, jnp.maximum(i * halo_blocks - 1, 0), 0))
    st_spec = pl.BlockSpec((bb, CONV_HALO, CONV_CH), lambda b, i: (b, 0, 0))
    return pl.pallas_call(
        _mix_kernel,
        grid=grid,
        in_specs=[row(D_MODEL), row(ATTN_W), row(CONV_CH), halo, st_spec, row(D_MODEL), row(D_MODEL),
                  _const_spec((CONV_W, SUBLANES, CONV_CH)), _const_spec((1, CONV_CH)), _const_spec((1, CONV_CH)),
                  _const_spec((1, CONV_CH)), _const_spec((CONV_CH, D_MODEL)), _const_spec((1, D_MODEL)),
                  _const_spec((ATTN_W, D_MODEL)), _const_spec((D_MODEL, D_MODEL))],
        out_specs=row(D_MODEL),
        out_shape=jax.ShapeDtypeStruct((nb, t, D_MODEL), F32),
        scratch_shapes=[pltpu.VMEM((SUBLANES, bb, CONV_HALO + tm, CONV_CH), F32),
                        pltpu.VMEM((bb * tm, CONV_CH), BF16)],
        compiler_params=pltpu.CompilerParams(
            dimension_semantics=("arbitrary", "arbitrary"), vmem_limit_bytes=VMEM_LIMIT),
        name="mix",
    )(x, a, u, u, state, sga, sgc, jnp.broadcast_to(dww[:, None, :], (CONV_W, SUBLANES, CONV_CH)),
      dwb, cng, cnb, wco, bco, wao, wo)


def _ffn_kernel(x1_ref, st_ref, g2_ref, wup_ref, fdw_ref, fdb_ref, wdn_ref, fg_ref,
                y_ref, nf_ref, prev_ref, extg_ref, extv_ref, act_ref, *, final_norm):
    i = pl.program_id(1)
    bb, tm, _ = x1_ref.shape
    rows = bb * tm

    @pl.when(i == 0)
    def _():
        prev_ref[...] = st_ref[...]

    x1 = x1_ref[...].reshape(rows, D_MODEL)
    h = x1 * lax.rsqrt(jnp.mean(x1 * x1, axis=-1, keepdims=True) + EPS) * g2_ref[...]
    h = h.astype(BF16)

    def conv(ext_ref, c0):
        up = jnp.dot(h, wup_ref[:, c0:c0 + FFN_CHUNK], preferred_element_type=F32)
        up = up.reshape(bb, tm, FFN_CHUNK)
        ext_ref[:, 0:FFN_HALO, :] = prev_ref[:, :, c0:c0 + FFN_CHUNK]
        ext_ref[:, FFN_HALO:FFN_HALO + tm, :] = up
        tail = up[:, tm - FFN_HALO:tm, :]
        prev_ref[:, :, c0:c0 + FFN_CHUNK] = tail
        nf_ref[:, :, c0:c0 + FFN_CHUNK] = tail
        w = fdw_ref[:, c0:c0 + FFN_CHUNK]
        out = (ext_ref[:, FFN_HALO - 2:FFN_HALO - 2 + tm, :] * w[0:1]
               + ext_ref[:, FFN_HALO - 1:FFN_HALO - 1 + tm, :] * w[1:2]
               + up * w[2:3] + fdb_ref[:, c0:c0 + FFN_CHUNK])
        return out.reshape(rows, FFN_CHUNK)

    for c in range(D_FF // FFN_CHUNK):
        gate = conv(extg_ref, c * FFN_CHUNK)
        val = conv(extv_ref, D_FF + c * FFN_CHUNK)
        act_ref[:, c * FFN_CHUNK:(c + 1) * FFN_CHUNK] = (gate * _sigmoid(gate) * val).astype(BF16)

    y = x1 + jnp.dot(act_ref[...], wdn_ref[...], preferred_element_type=F32)
    if final_norm:
        y = y * lax.rsqrt(jnp.mean(y * y, axis=-1, keepdims=True) + EPS) * fg_ref[...]
    y_ref[...] = y.reshape(bb, tm, D_MODEL)


def _ffn(x1, state, g2, wup, fdw, fdb, wdn, fg, *, bb, tm, final_norm):
    nb, t, _ = x1.shape
    grid = (nb // bb, t // tm)
    row = pl.BlockSpec((bb, tm, D_MODEL), lambda b, i: (b, i, 0))
    st_spec = pl.BlockSpec((bb, FFN_HALO, 2 * D_FF), lambda b, i: (b, 0, 0))
    return pl.pallas_call(
        functools.partial(_ffn_kernel, final_norm=final_norm),
        grid=grid,
        in_specs=[row, st_spec, _const_spec((1, D_MODEL)), _const_spec((D_MODEL, 2 * D_FF)),
                  _const_spec((FFN_CONV_W, 2 * D_FF)), _const_spec((1, 2 * D_FF)),
                  _const_spec((D_FF, D_MODEL)), _const_spec((1, D_MODEL))],
        out_specs=[row, st_spec],
        out_shape=[jax.ShapeDtypeStruct((nb, t, D_MODEL), F32),
                   jax.ShapeDtypeStruct((nb, FFN_HALO, 2 * D_FF), F32)],
        scratch_shapes=[pltpu.VMEM((bb, FFN_HALO, 2 * D_FF), F32),
                        pltpu.VMEM((bb, FFN_HALO + tm, FFN_CHUNK), F32),
                        pltpu.VMEM((bb, FFN_HALO + tm, FFN_CHUNK), F32),
                        pltpu.VMEM((bb * tm, D_FF), BF16)],
        compiler_params=pltpu.CompilerParams(
            dimension_semantics=("arbitrary", "arbitrary"), vmem_limit_bytes=VMEM_LIMIT),
        name="ffn",
    )(x1, state, g2, wup, fdw, fdb, wdn, fg)


def _cumsum_matrix():
    j = np.arange(2 * KEY_BLOCK)[:, None]
    s = np.arange(2 * KEY_BLOCK)[None, :]
    same = (j // KEY_BLOCK) == (s // KEY_BLOCK)
    return (same & ((j % KEY_BLOCK) > (s % KEY_BLOCK))).astype(np.float32)


def _head_mean_matrix():
    h = np.arange(ATTN_W) // HEAD_DIM
    return (h[:, None] == h[None, :]).astype(np.float32)


def _pad_rows_front(a, rows):
    return jnp.pad(a, ((0, 0), (rows - a.shape[1], 0), (0, 0)))


def kernel(x_prompt, x_sample, cache_sb_k, cache_sb_v, state_conv, state_ffn_conv, norm1_g, w_in, b_in,
           q_norm_g, k_norm_g, w_attn_out, dw_w, dw_b, cn_g, cn_b, w_conv_out, b_conv_out, w_out,
           norm2_g, w_up, ffn_dw_w, ffn_dw_b, w_down, final_g):
    depth = w_in.shape[0]
    bp, seq, _ = x_prompt.shape
    bs, dec, _ = x_sample.shape
    past = cache_sb_k.shape[2]
    um = jnp.asarray(_cumsum_matrix(), BF16)
    hm = jnp.asarray(_head_mean_matrix(), BF16)
    row2 = lambda a: a.reshape(1, -1)

    past_k = cache_sb_k.reshape(depth, bs, past, ATTN_W).astype(BF16)
    past_v = cache_sb_v.reshape(depth, bs, past, ATTN_W).astype(BF16)
    conv_zero = jnp.zeros((bp, CONV_HALO, CONV_CH), F32)
    ffn_zero = jnp.zeros((bp, FFN_HALO, 2 * D_FF), F32)

    xp, xs = x_prompt, x_sample
    outs = {name: [] for name in ("kp", "vp", "ks", "vs", "cp", "cs", "fp", "fs")}
    for l in range(depth):
        win = w_in[l].astype(BF16)
        wao = w_attn_out[l].astype(BF16)
        wco = w_conv_out[l].astype(BF16)
        wo = w_out[l].astype(BF16)
        wup = w_up[l].astype(BF16)
        wdn = w_down[l].astype(BF16)
        qg = row2(jnp.tile(q_norm_g[l], N_HEADS))
        kg = row2(jnp.tile(k_norm_g[l], N_HEADS))
        final_norm = l == depth - 1

        def run(x, conv_state, ffn_state, k_past, v_past, bb, tm, tq):
            q, k, v, kbt, vbd, u, sga, sgc = _inproj(x, row2(norm1_g[l]), win, row2(b_in[l]), qg, kg, hm,
                                                     bb=bb, tm=tm)
            t = x.shape[1]
            a = _attention(q, kbt, vbd, k_past, v_past, um, tq=tq)
            x1 = _mix(x, a, u, conv_state, sga, sgc, dw_w[l], row2(dw_b[l]), row2(cn_g[l]), row2(cn_b[l]),
                      wco, row2(b_conv_out[l]), wao, wo, bb=bb, tm=tm)
            y, nf = _ffn(x1, ffn_state, row2(norm2_g[l]), wup, ffn_dw_w[l], row2(ffn_dw_b[l]), wdn,
                         row2(final_g), bb=bb, tm=tm, final_norm=final_norm)
            return y, k, v, u[:, t - (CONV_W - 1):, :], nf[:, FFN_HALO - (FFN_CONV_W - 1):, :]

        xp, k1, v1, c1, f1 = run(xp, conv_zero, ffn_zero, None, None, 1, 256, 256)
        xs, k2, v2, c2, f2 = run(xs, _pad_rows_front(state_conv[l], CONV_HALO),
                                 _pad_rows_front(state_ffn_conv[l], FFN_HALO),
                                 past_k[l], past_v[l], bs, dec, dec)
        for name, val in zip(("kp", "vp", "ks", "vs", "cp", "cs", "fp", "fs"), (k1, v1, k2, v2, c1, c2, f1, f2)):
            outs[name].append(val)

    heads = lambda lst, b, t: jnp.stack(lst).reshape(depth, b, t, N_HEADS, HEAD_DIM)
    return (xp, xs, heads(outs["kp"], bp, seq), heads(outs["vp"], bp, seq),
            heads(outs["ks"], bs, dec), heads(outs["vs"], bs, dec),
            jnp.stack(outs["cp"]), jnp.stack(outs["cs"]), jnp.stack(outs["fp"]), jnp.stack(outs["fs"]))
```

```python
import functools
from typing import NamedTuple

import numpy as np
import jax
import jax.numpy as jnp
from jax import lax
from jax.experimental import pallas as pl
from jax.experimental.pallas import tpu as pltpu

D_MODEL = 1024
N_HEADS = 8
HEAD_DIM = 64
ATTN_W = N_HEADS * HEAD_DIM
CONV_CH = 512
CONV_W = 31
D_FF = 2816
FFN_CONV_W = 3
N_IN = 3 * ATTN_W + 2 * CONV_CH + 2 * D_MODEL
EPS = 1e-6

LANES = 128
SUBLANES = 8
KEY_BLOCK = 128
HEAD_PAIRS = ATTN_W // LANES
CONV_HALO = 32
FFN_HALO = SUBLANES
FFN_CHUNK = 256


class Tiles(NamedTuple):
    batch: int
    proj: int
    attn: int
    mix: int
    ffn: int


PROMPT_TILES = Tiles(batch=1, proj=512, attn=256, mix=256, ffn=256)
VMEM_LIMIT = 56 * 1024 * 1024
LOG2E = 1.4426950408889634
CARRY_LIMIT = 152.0

F32 = jnp.float32
BF16 = jnp.bfloat16


def _const_spec(shape):
    return pl.BlockSpec(shape, lambda *_: (0,) * len(shape), pipeline_mode=pl.Buffered(1))


def _sigmoid(x):
    return 1.0 / (1.0 + jnp.exp(-x))


def _inproj_kernel(x_ref, g1_ref, win_ref, bin_ref, qg_ref, kg_ref, hm_ref,
                   q_ref, k_ref, v_ref, kbt_ref, vbd_ref, u_ref, sga_ref, sgc_ref, *, feature_major):
    bb, tm, _ = x_ref.shape
    rows = bb * tm
    x = x_ref[...].reshape(rows, D_MODEL)
    h = x * lax.rsqrt(jnp.mean(x * x, axis=-1, keepdims=True) + EPS) * g1_ref[...]
    h = h.astype(BF16)

    def proj(c0, width):
        return (jnp.dot(h, win_ref[:, c0:c0 + width], preferred_element_type=F32)
                + bin_ref[:, c0:c0 + width])

    def head_rmsnorm(y, g):
        sq = y * y
        hi = sq.astype(BF16)
        lo = (sq - hi.astype(F32)).astype(BF16)
        ssum = (jnp.dot(hi, hm_ref[...], preferred_element_type=F32)
                + jnp.dot(lo, hm_ref[...], preferred_element_type=F32))
        return y * lax.rsqrt(ssum * (1.0 / HEAD_DIM) + EPS) * g

    q = head_rmsnorm(proj(0, ATTN_W), qg_ref[...])
    q_ref[...] = (q * (HEAD_DIM ** -0.5 * LOG2E)).astype(BF16).reshape(bb, tm, ATTN_W)
    k = head_rmsnorm(proj(ATTN_W, ATTN_W), kg_ref[...])
    v = proj(2 * ATTN_W, ATTN_W)
    if feature_major:
        for b in range(bb):
            k_ref[b] = k[b * tm:(b + 1) * tm].T
            v_ref[b] = v[b * tm:(b + 1) * tm].T
    else:
        k_ref[...] = k.reshape(bb, tm, ATTN_W)
        v_ref[...] = v.reshape(bb, tm, ATTN_W)

    blk = min(tm, KEY_BLOCK)
    even_row = (lax.broadcasted_iota(jnp.int32, (ATTN_W, KEY_BLOCK), 0) & (LANES - 1)) < HEAD_DIM
    even_lane = (lax.broadcasted_iota(jnp.int32, (KEY_BLOCK, ATTN_W), 1) & (LANES - 1)) < HEAD_DIM
    for b in range(bb):
        for c in range(tm // blk):
            r0 = b * tm + c * blk
            kb, vb = k[r0:r0 + blk], v[r0:r0 + blk]
            if blk < KEY_BLOCK:
                fill = jnp.zeros((KEY_BLOCK - blk, ATTN_W), F32)
                kb, vb = jnp.concatenate([kb, fill], axis=0), jnp.concatenate([vb, fill], axis=0)
            kt = kb.T
            o = 2 * c * KEY_BLOCK
            kbt_ref[b, :, o:o + KEY_BLOCK] = jnp.where(even_row, kt, 0.0).astype(BF16)
            kbt_ref[b, :, o + KEY_BLOCK:o + 2 * KEY_BLOCK] = jnp.where(even_row, 0.0, kt).astype(BF16)
            vbd_ref[b, o:o + KEY_BLOCK, :] = jnp.where(even_lane, vb, 0.0).astype(BF16)
            vbd_ref[b, o + KEY_BLOCK:o + 2 * KEY_BLOCK, :] = jnp.where(even_lane, 0.0, vb).astype(BF16)

    c0 = 3 * ATTN_W
    u = proj(c0, CONV_CH) * _sigmoid(proj(c0 + CONV_CH, CONV_CH))
    u_ref[...] = u.reshape(bb, tm, CONV_CH)
    c0 += 2 * CONV_CH
    sga_ref[...] = _sigmoid(proj(c0, D_MODEL)).astype(BF16).reshape(bb, tm, D_MODEL)
    sgc_ref[...] = _sigmoid(proj(c0 + D_MODEL, D_MODEL)).astype(BF16).reshape(bb, tm, D_MODEL)


def _inproj(x, g1, win, b_in, qg, kg, hm, *, bb, tm, feature_major):
    nb, t, _ = x.shape
    grid = (nb // bb, t // tm)
    row = lambda width: pl.BlockSpec((bb, tm, width), lambda b, i: (b, i, 0))
    shp = lambda width, dt: jax.ShapeDtypeStruct((nb, t, width), dt)
    kw = 2 * KEY_BLOCK * (tm // min(tm, KEY_BLOCK))
    if feature_major:
        kv_spec = pl.BlockSpec((bb, ATTN_W, tm), lambda b, i: (b, 0, i))
        kv_shape = jax.ShapeDtypeStruct((nb, ATTN_W, t), F32)
    else:
        kv_spec, kv_shape = row(ATTN_W), shp(ATTN_W, F32)
    return pl.pallas_call(
        functools.partial(_inproj_kernel, feature_major=feature_major),
        grid=grid,
        in_specs=[row(D_MODEL), _const_spec((1, D_MODEL)), _const_spec((D_MODEL, N_IN)),
                  _const_spec((1, N_IN)), _const_spec((1, ATTN_W)), _const_spec((1, ATTN_W)),
                  _const_spec((ATTN_W, ATTN_W))],
        out_specs=[row(ATTN_W), kv_spec, kv_spec,
                   pl.BlockSpec((bb, ATTN_W, kw), lambda b, i: (b, 0, i)),
                   pl.BlockSpec((bb, kw, ATTN_W), lambda b, i: (b, i, 0)),
                   row(CONV_CH), row(D_MODEL), row(D_MODEL)],
        out_shape=[shp(ATTN_W, BF16), kv_shape, kv_shape,
                   jax.ShapeDtypeStruct((nb, ATTN_W, kw * (t // tm)), BF16),
                   jax.ShapeDtypeStruct((nb, kw * (t // tm), ATTN_W), BF16),
                   shp(CONV_CH, F32), shp(D_MODEL, BF16), shp(D_MODEL, BF16)],
        compiler_params=pltpu.CompilerParams(
            dimension_semantics=("arbitrary", "arbitrary"), vmem_limit_bytes=VMEM_LIMIT),
        name="inproj",
    )(x, g1, win, b_in, qg, kg, hm)


def _attn_kernel(*refs, tq, n_masked, n_past, layer):
    if n_past:
        (q_ref, kn_ref, vn_ref, um_ref, kp_ref, vp_ref, o_ref,
         carry_ref, acc_ref, sp_ref, ls_ref, kbuf_ref, vbuf_ref, sem_ref) = refs
    else:
        q_ref, kn_ref, vn_ref, um_ref, o_ref, carry_ref, acc_ref, sp_ref, ls_ref = refs
    batch = pl.program_id(0)
    i = pl.program_id(1)
    blocks_per_tile = tq // KEY_BLOCK
    carry_ref[...] = jnp.zeros_like(carry_ref)
    acc_ref[...] = jnp.zeros_like(acc_ref)

    def step(scores, weighted, mask):
        for p in range(HEAD_PAIRS):
            z = scores(p)
            sp = jnp.where(z > 64.0, z, jnp.log(1.0 + jnp.exp2(z)) * LOG2E)
            if mask is not None:
                sp = jnp.where(mask, sp, 0.0)
            sp_ref[p * tq:(p + 1) * tq, :] = sp.astype(BF16)
            ls_ref[p * tq:(p + 1) * tq, :] = z - sp
        later_all = jnp.dot(sp_ref[...], um_ref[...], preferred_element_type=F32)
        smallest = None
        for p in range(HEAD_PAIRS):
            later = later_all[p * tq:(p + 1) * tq]
            w = jnp.exp2(ls_ref[p * tq:(p + 1) * tq, :] - later - carry_ref[p])
            if mask is not None:
                w = jnp.where(mask, w, 0.0)
            acc_ref[p] += weighted(p, w.astype(BF16))
            halves = []
            for c0 in (0, KEY_BLOCK):
                own = sp_ref[p * tq:(p + 1) * tq, c0:c0 + KEY_BLOCK].astype(F32)
                halves.append(jnp.broadcast_to(later[:, c0:c0 + 1] + own[:, 0:1], (tq, KEY_BLOCK)))
            carry = carry_ref[p] + jnp.concatenate(halves, axis=1)
            carry_ref[p] = carry
            smallest = carry if smallest is None else jnp.minimum(smallest, carry)
        return jnp.min(smallest)

    def new_block(jb):
        start = pl.multiple_of(jb * 2 * KEY_BLOCK, 2 * KEY_BLOCK)
        scores = lambda p: jnp.dot(q_ref[0, :, p * LANES:(p + 1) * LANES],
                                   kn_ref[0, p * LANES:(p + 1) * LANES, pl.ds(start, 2 * KEY_BLOCK)],
                                   preferred_element_type=F32)
        weighted = lambda p, w: jnp.dot(w, vn_ref[0, pl.ds(start, 2 * KEY_BLOCK), p * LANES:(p + 1) * LANES],
                                        preferred_element_type=F32)
        return scores, weighted

    def past_block(slot):
        even_feature = lax.broadcasted_iota(jnp.int32, (LANES, KEY_BLOCK), 0) < HEAD_DIM

        def split(buf_ref, p):
            blk = buf_ref[slot, p * LANES:(p + 1) * LANES, :]
            return jnp.concatenate([jnp.where(even_feature, blk, 0.0).astype(BF16),
                                    jnp.where(even_feature, 0.0, blk).astype(BF16)], axis=1)

        scores = lambda p: jnp.dot(q_ref[0, :, p * LANES:(p + 1) * LANES], split(kbuf_ref, p),
                                   preferred_element_type=F32)
        weighted = lambda p, w: lax.dot_general(w, split(vbuf_ref, p), (((1,), (1,)), ((), ())),
                                                preferred_element_type=F32)
        return scores, weighted

    def past_copies(n, slot):
        start = pl.multiple_of((n_past - 1 - n) * KEY_BLOCK, KEY_BLOCK)
        return [pltpu.make_async_copy(src.at[layer, batch, :, pl.ds(start, KEY_BLOCK)], dst.at[slot],
                                      sem_ref.at[which, slot])
                for which, (src, dst) in enumerate(((kp_ref, kbuf_ref), (vp_ref, vbuf_ref)))]

    row = lax.broadcasted_iota(jnp.int32, (tq, 2 * KEY_BLOCK), 0)
    col = lax.broadcasted_iota(jnp.int32, (tq, 2 * KEY_BLOCK), 1) & (KEY_BLOCK - 1)
    qpos = i * tq + row
    smallest = jnp.float32(0.0)
    for d in range(n_masked):
        jb = i * blocks_per_tile + (n_masked - 1 - d)
        smallest = step(*new_block(jb), (jb * KEY_BLOCK + col) < qpos)

    def unfinished(n_blocks):
        return lambda state: jnp.logical_and(state[0] < n_blocks, state[1] < CARRY_LIMIT)

    n_open = i * blocks_per_tile
    _, smallest = lax.while_loop(
        unfinished(n_open), lambda state: (state[0] + 1, step(*new_block(n_open - 1 - state[0]), None)),
        (jnp.int32(0), smallest))

    if n_past:
        for copy in past_copies(0, 0):
            copy.start()

        def past_body(state):
            n = state[0]
            slot = n & 1
            for copy in past_copies(n, slot):
                copy.wait()

            @pl.when(n + 1 < n_past)
            def _():
                for copy in past_copies(n + 1, 1 - slot):
                    copy.start()

            return n + 1, step(*past_block(slot), None)

        n_used, _ = lax.while_loop(unfinished(n_past), past_body, (jnp.int32(0), smallest))

        @pl.when(n_used < n_past)
        def _():
            for copy in past_copies(n_used, n_used & 1):
                copy.wait()

    for p in range(HEAD_PAIRS):
        o_ref[0, :, p * LANES:(p + 1) * LANES] = acc_ref[p].astype(BF16)


def _attention(q, kbt_new, vbd_new, um, *, tq, kt_cache=None, vt_cache=None, layer=0):
    nb, t, _ = q.shape
    tn2 = vbd_new.shape[1]
    n_masked = max(1, tq // KEY_BLOCK)
    n_past = 0 if kt_cache is None else kt_cache.shape[3] // KEY_BLOCK
    grid = (nb, t // tq)
    q_spec = pl.BlockSpec((1, tq, ATTN_W), lambda b, i: (b, i, 0))
    in_specs = [q_spec, pl.BlockSpec((1, ATTN_W, tn2), lambda b, i: (b, 0, 0)),
                pl.BlockSpec((1, tn2, ATTN_W), lambda b, i: (b, 0, 0)), _const_spec(um.shape)]
    args = [q, kbt_new, vbd_new, um]
    scratch = [pltpu.VMEM((HEAD_PAIRS, tq, 2 * KEY_BLOCK), F32),
               pltpu.VMEM((HEAD_PAIRS, tq, LANES), F32),
               pltpu.VMEM((HEAD_PAIRS * tq, 2 * KEY_BLOCK), BF16),
               pltpu.VMEM((HEAD_PAIRS * tq, 2 * KEY_BLOCK), F32)]
    if n_past:
        assert kt_cache.shape[3] % KEY_BLOCK == 0 and kt_cache.shape[1:3] == (nb, ATTN_W)
        in_specs += [pl.BlockSpec(memory_space=pl.ANY)] * 2
        args += [kt_cache, vt_cache]
        scratch += [pltpu.VMEM((2, ATTN_W, KEY_BLOCK), F32), pltpu.VMEM((2, ATTN_W, KEY_BLOCK), F32),
                    pltpu.SemaphoreType.DMA((2, 2))]
    return pl.pallas_call(
        functools.partial(_attn_kernel, tq=tq, n_masked=n_masked, n_past=n_past, layer=layer),
        grid=grid,
        in_specs=in_specs,
        out_specs=q_spec,
        out_shape=jax.ShapeDtypeStruct((nb, t, ATTN_W), BF16),
        scratch_shapes=scratch,
        compiler_params=pltpu.CompilerParams(
            dimension_semantics=("arbitrary",) * 2, vmem_limit_bytes=VMEM_LIMIT),
        name="attention",
    )(*args)


def _mix_kernel(x_ref, a_ref, u_ref, uh_ref, st_ref, sga_ref, sgc_ref, dww_ref, dwb_ref, cng_ref,
                cnb_ref, wco_ref, bco_ref, wao_ref, wo_ref, x1_ref, ext_ref, act_ref):
    i = pl.program_id(1)
    bb, tm, _ = x_ref.shape
    rows = bb * tm

    @pl.when(i == 0)
    def _():
        ext_ref[0, :, 0:CONV_HALO, :] = st_ref[...]

    @pl.when(i > 0)
    def _():
        ext_ref[0, :, 0:CONV_HALO, :] = uh_ref[...]

    ext_ref[0, :, CONV_HALO:CONV_HALO + tm, :] = u_ref[...]
    span = CONV_HALO + tm - SUBLANES
    for s in range(1, SUBLANES):
        ext_ref[s, :, 0:span, :] = ext_ref[0, :, s:s + span, :]

    chunk = 32
    first = CONV_HALO - (CONV_W - 1)
    for b in range(bb):
        for r0 in range(0, tm, chunk):
            acc = jnp.broadcast_to(dwb_ref[...], (chunk // SUBLANES, SUBLANES, CONV_CH))
            for j in range(CONV_W):
                s, base = (first + j) % SUBLANES, (first + j) // SUBLANES * SUBLANES
                taps = ext_ref[s, b, r0 + base:r0 + base + chunk, :]
                acc = acc + taps.reshape(chunk // SUBLANES, SUBLANES, CONV_CH) * dww_ref[j]
            acc = acc.reshape(chunk, CONV_CH)
            mu = jnp.mean(acc, axis=-1, keepdims=True)
            xc = acc - mu
            y = xc * lax.rsqrt(jnp.mean(xc * xc, axis=-1, keepdims=True) + EPS)
            y = y * cng_ref[...] + cnb_ref[...]
            act_ref[b * tm + r0:b * tm + r0 + chunk, :] = (y * _sigmoid(y)).astype(BF16)

    c = jnp.dot(act_ref[...], wco_ref[...], preferred_element_type=F32) + bco_ref[...]
    a = jnp.dot(a_ref[...].reshape(rows, ATTN_W), wao_ref[...], preferred_element_type=F32)
    m = (sga_ref[...].reshape(rows, D_MODEL).astype(F32) * a
         + sgc_ref[...].reshape(rows, D_MODEL).astype(F32) * c)
    x1 = x_ref[...].reshape(rows, D_MODEL) + jnp.dot(m.astype(BF16), wo_ref[...],
                                                     preferred_element_type=F32)
    x1_ref[...] = x1.reshape(bb, tm, D_MODEL)


def _mix(x, a, u, state, sga, sgc, dww, dwb, cng, cnb, wco, bco, wao, wo, *, bb, tm):
    nb, t, _ = x.shape
    grid = (nb // bb, t // tm)
    row = lambda width: pl.BlockSpec((bb, tm, width), lambda b, i: (b, i, 0))
    halo_blocks = tm // CONV_HALO
    halo = pl.BlockSpec((bb, CONV_HALO, CONV_CH),
                        lambda b, i: (b, jnp.maximum(i * halo_blocks - 1, 0), 0))
    st_spec = pl.BlockSpec((bb, CONV_HALO, CONV_CH), lambda b, i: (b, 0, 0))
    return pl.pallas_call(
        _mix_kernel,
        grid=grid,
        in_specs=[row(D_MODEL), row(ATTN_W), row(CONV_CH), halo, st_spec, row(D_MODEL), row(D_MODEL),
                  _const_spec((CONV_W, SUBLANES, CONV_CH)), _const_spec((1, CONV_CH)), _const_spec((1, CONV_CH)),
                  _const_spec((1, CONV_CH)), _const_spec((CONV_CH, D_MODEL)), _const_spec((1, D_MODEL)),
                  _const_spec((ATTN_W, D_MODEL)), _const_spec((D_MODEL, D_MODEL))],
        out_specs=row(D_MODEL),
        out_shape=jax.ShapeDtypeStruct((nb, t, D_MODEL), F32),
        scratch_shapes=[pltpu.VMEM((SUBLANES, bb, CONV_HALO + tm, CONV_CH), F32),
                        pltpu.VMEM((bb * tm, CONV_CH), BF16)],
        compiler_params=pltpu.CompilerParams(
            dimension_semantics=("arbitrary", "arbitrary"), vmem_limit_bytes=VMEM_LIMIT),
        name="mix",
    )(x, a, u, u, state, sga, sgc, jnp.broadcast_to(dww[:, None, :], (CONV_W, SUBLANES, CONV_CH)),
      dwb, cng, cnb, wco, bco, wao, wo)


def _ffn_kernel(x1_ref, st_ref, g2_ref, wup_ref, fdw_ref, fdb_ref, wdn_ref, fg_ref,
                y_ref, nf_ref, prev_ref, extg_ref, extv_ref, act_ref, *, final_norm):
    i = pl.program_id(1)
    bb, tm, _ = x1_ref.shape
    rows = bb * tm

    @pl.when(i == 0)
    def _():
        prev_ref[...] = st_ref[...]

    x1 = x1_ref[...].reshape(rows, D_MODEL)
    h = x1 * lax.rsqrt(jnp.mean(x1 * x1, axis=-1, keepdims=True) + EPS) * g2_ref[...]
    h = h.astype(BF16)

    def conv(ext_ref, c0):
        up = jnp.dot(h, wup_ref[:, c0:c0 + FFN_CHUNK], preferred_element_type=F32)
        up = up.reshape(bb, tm, FFN_CHUNK)
        ext_ref[:, 0:FFN_HALO, :] = prev_ref[:, :, c0:c0 + FFN_CHUNK]
        ext_ref[:, FFN_HALO:FFN_HALO + tm, :] = up
        tail = up[:, tm - FFN_HALO:tm, :]
        prev_ref[:, :, c0:c0 + FFN_CHUNK] = tail
        nf_ref[:, :, c0:c0 + FFN_CHUNK] = tail
        w = fdw_ref[:, c0:c0 + FFN_CHUNK]
        out = (ext_ref[:, FFN_HALO - 2:FFN_HALO - 2 + tm, :] * w[0:1]
               + ext_ref[:, FFN_HALO - 1:FFN_HALO - 1 + tm, :] * w[1:2]
               + up * w[2:3] + fdb_ref[:, c0:c0 + FFN_CHUNK])
        return out.reshape(rows, FFN_CHUNK)

    for c in range(D_FF // FFN_CHUNK):
        gate = conv(extg_ref, c * FFN_CHUNK)
        val = conv(extv_ref, D_FF + c * FFN_CHUNK)
        act_ref[:, c * FFN_CHUNK:(c + 1) * FFN_CHUNK] = (gate * _sigmoid(gate) * val).astype(BF16)

    y = x1 + jnp.dot(act_ref[...], wdn_ref[...], preferred_element_type=F32)
    if final_norm:
        y = y * lax.rsqrt(jnp.mean(y * y, axis=-1, keepdims=True) + EPS) * fg_ref[...]
    y_ref[...] = y.reshape(bb, tm, D_MODEL)


def _ffn(x1, state, g2, wup, fdw, fdb, wdn, fg, *, bb, tm, final_norm):
    nb, t, _ = x1.shape
    grid = (nb // bb, t // tm)
    row = pl.BlockSpec((bb, tm, D_MODEL), lambda b, i: (b, i, 0))
    st_spec = pl.BlockSpec((bb, FFN_HALO, 2 * D_FF), lambda b, i: (b, 0, 0))
    return pl.pallas_call(
        functools.partial(_ffn_kernel, final_norm=final_norm),
        grid=grid,
        in_specs=[row, st_spec, _const_spec((1, D_MODEL)), _const_spec((D_MODEL, 2 * D_FF)),
                  _const_spec((FFN_CONV_W, 2 * D_FF)), _const_spec((1, 2 * D_FF)),
                  _const_spec((D_FF, D_MODEL)), _const_spec((1, D_MODEL))],
        out_specs=[row, st_spec],
        out_shape=[jax.ShapeDtypeStruct((nb, t, D_MODEL), F32),
                   jax.ShapeDtypeStruct((nb, FFN_HALO, 2 * D_FF), F32)],
        scratch_shapes=[pltpu.VMEM((bb, FFN_HALO, 2 * D_FF), F32),
                        pltpu.VMEM((bb, FFN_HALO + tm, FFN_CHUNK), F32),
                        pltpu.VMEM((bb, FFN_HALO + tm, FFN_CHUNK), F32),
                        pltpu.VMEM((bb * tm, D_FF), BF16)],
        compiler_params=pltpu.CompilerParams(
            dimension_semantics=("arbitrary", "arbitrary"), vmem_limit_bytes=VMEM_LIMIT),
        name="ffn",
    )(x1, state, g2, wup, fdw, fdb, wdn, fg)


def _cumsum_matrix():
    j = np.arange(2 * KEY_BLOCK)[:, None]
    s = np.arange(2 * KEY_BLOCK)[None, :]
    same = (j // KEY_BLOCK) == (s // KEY_BLOCK)
    return (same & ((j % KEY_BLOCK) > (s % KEY_BLOCK))).astype(np.float32)


def _head_mean_matrix():
    h = np.arange(ATTN_W) // HEAD_DIM
    return (h[:, None] == h[None, :]).astype(np.float32)


def _pad_rows_front(a, rows):
    return jnp.pad(a, ((0, 0), (rows - a.shape[1], 0), (0, 0)))


def kernel(x_prompt, x_sample, cache_sb_k, cache_sb_v, state_conv, state_ffn_conv, norm1_g, w_in, b_in,
           q_norm_g, k_norm_g, w_attn_out, dw_w, dw_b, cn_g, cn_b, w_conv_out, b_conv_out, w_out,
           norm2_g, w_up, ffn_dw_w, ffn_dw_b, w_down, final_g):
    depth = w_in.shape[0]
    bp, seq, _ = x_prompt.shape
    bs, dec, _ = x_sample.shape
    past = cache_sb_k.shape[2]
    um = jnp.asarray(_cumsum_matrix(), BF16)
    hm = jnp.asarray(_head_mean_matrix(), BF16)
    row2 = lambda a: a.reshape(1, -1)

    feature_major = lambda c: jnp.transpose(c, (0, 1, 3, 4, 2)).reshape(depth, bs, ATTN_W, past)
    kt_cache, vt_cache = feature_major(cache_sb_k), feature_major(cache_sb_v)
    conv_zero = jnp.zeros((bp, CONV_HALO, CONV_CH), F32)
    ffn_zero = jnp.zeros((bp, FFN_HALO, 2 * D_FF), F32)

    xp, xs = x_prompt, x_sample
    outs = {name: [] for name in ("kp", "vp", "ks", "vs", "cp", "cs", "fp", "fs")}
    for l in range(depth):
        win = w_in[l].astype(BF16)
        wao = w_attn_out[l].astype(BF16)
        wco = w_conv_out[l].astype(BF16)
        wo = w_out[l].astype(BF16)
        wup = w_up[l].astype(BF16)
        wdn = w_down[l].astype(BF16)
        qg = row2(jnp.tile(q_norm_g[l], N_HEADS))
        kg = row2(jnp.tile(k_norm_g[l], N_HEADS))
        final_norm = l == depth - 1

        def run(x, conv_state, ffn_state, cached, tiles):
            bb = tiles.batch
            q, k, v, kbt, vbd, u, sga, sgc = _inproj(x, row2(norm1_g[l]), win, row2(b_in[l]), qg, kg, hm,
                                                     bb=bb, tm=tiles.proj, feature_major=not cached)
            t = x.shape[1]
            if cached:
                a = _attention(q, kbt, vbd, um, tq=tiles.attn, kt_cache=kt_cache, vt_cache=vt_cache, layer=l)
            else:
                a = _attention(q, kbt, vbd, um, tq=tiles.attn)
            x1 = _mix(x, a, u, conv_state, sga, sgc, dw_w[l], row2(dw_b[l]), row2(cn_g[l]), row2(cn_b[l]),
                      wco, row2(b_conv_out[l]), wao, wo, bb=bb, tm=tiles.mix)
            y, nf = _ffn(x1, ffn_state, row2(norm2_g[l]), wup, ffn_dw_w[l], row2(ffn_dw_b[l]), wdn,
                         row2(final_g), bb=bb, tm=tiles.ffn, final_norm=final_norm)
            return y, k, v, u[:, t - (CONV_W - 1):, :], nf[:, FFN_HALO - (FFN_CONV_W - 1):, :]

        xp, k1, v1, c1, f1 = run(xp, conv_zero, ffn_zero, False, PROMPT_TILES)
        xs, k2, v2, c2, f2 = run(xs, _pad_rows_front(state_conv[l], CONV_HALO),
                                 _pad_rows_front(state_ffn_conv[l], FFN_HALO), True,
                                 Tiles(batch=bs, proj=dec, attn=dec, mix=dec, ffn=dec))
        for name, val in zip(("kp", "vp", "ks", "vs", "cp", "cs", "fp", "fs"), (k1, v1, k2, v2, c1, c2, f1, f2)):
            outs[name].append(val)

    heads = lambda lst, b, t: jnp.stack(lst).reshape(depth, b, t, N_HEADS, HEAD_DIM)
    heads_t = lambda lst: jnp.transpose(jnp.stack(lst).reshape(depth, bp, N_HEADS, HEAD_DIM, seq),
                                        (0, 1, 4, 2, 3))
    return (xp, xs, heads_t(outs["kp"]), heads_t(outs["vp"]),
            heads(outs["ks"], bs, dec), heads(outs["vs"], bs, dec),
            jnp.stack(outs["cp"]), jnp.stack(outs["cs"]), jnp.stack(outs["fp"]), jnp.stack(outs["fs"]))
```

```python
import functools
from typing import NamedTuple

import numpy as np
import jax
import jax.numpy as jnp
from jax import lax
from jax.experimental import pallas as pl
from jax.experimental.pallas import tpu as pltpu

D_MODEL = 1024
N_HEADS = 8
HEAD_DIM = 64
ATTN_W = N_HEADS * HEAD_DIM
CONV_CH = 512
CONV_W = 31
D_FF = 2816
FFN_CONV_W = 3
N_IN = 3 * ATTN_W + 2 * CONV_CH + 2 * D_MODEL
EPS = 1e-6

LANES = 128
SUBLANES = 8
KEY_BLOCK = 128
HEAD_PAIRS = ATTN_W // LANES
CONV_HALO = 32
FFN_HALO = SUBLANES
FFN_CHUNK = 256


class Tiles(NamedTuple):
    batch: int
    proj: int
    attn: int
    mix: int
    ffn: int


PROMPT_TILES = Tiles(batch=1, proj=512, attn=256, mix=256, ffn=256)
VMEM_LIMIT = 56 * 1024 * 1024
LOG2E = 1.4426950408889634
CARRY_LIMIT = 152.0

F32 = jnp.float32
BF16 = jnp.bfloat16


def _const_spec(shape):
    return pl.BlockSpec(shape, lambda *_: (0,) * len(shape), pipeline_mode=pl.Buffered(1))


def _layer_spec(shape, layer):
    return pl.BlockSpec((None,) + tuple(shape), lambda *_: (layer,) + (0,) * len(shape),
                        pipeline_mode=pl.Buffered(1))


def _sigmoid(x):
    return 1.0 / (1.0 + jnp.exp(-x))


def _inproj_kernel(x_ref, g1_ref, win_ref, bin_ref, qg_ref, kg_ref, hm_ref, *refs, feature_major):
    q_ref, k_ref, v_ref, kbt_ref, vbd_ref, u_ref, sga_ref, sgc_ref = refs[2:] if feature_major else refs
    bb, tm, _ = x_ref.shape
    rows = bb * tm
    x = x_ref[...].reshape(rows, D_MODEL)
    h = x * lax.rsqrt(jnp.mean(x * x, axis=-1, keepdims=True) + EPS) * g1_ref[...]
    h = h.astype(BF16)

    def proj(c0, width):
        return (jnp.dot(h, win_ref[:, c0:c0 + width], preferred_element_type=F32)
                + bin_ref[:, c0:c0 + width])

    def head_rmsnorm(y, g):
        ssum = jnp.dot((y * y).astype(BF16), hm_ref[...], preferred_element_type=F32)
        return y * lax.rsqrt(ssum * (1.0 / HEAD_DIM) + EPS) * g

    q = head_rmsnorm(proj(0, ATTN_W), qg_ref[...])
    q_ref[...] = (q * (HEAD_DIM ** -0.5 * LOG2E)).astype(BF16).reshape(bb, tm, ATTN_W)
    k = head_rmsnorm(proj(ATTN_W, ATTN_W), kg_ref[...])
    v = proj(2 * ATTN_W, ATTN_W)
    if feature_major:
        for b in range(bb):
            k_ref[b] = k[b * tm:(b + 1) * tm].T
            v_ref[b] = v[b * tm:(b + 1) * tm].T
    else:
        k_ref[...] = k.reshape(bb, tm, ATTN_W)
        v_ref[...] = v.reshape(bb, tm, ATTN_W)

    blk = min(tm, KEY_BLOCK)
    even_row = (lax.broadcasted_iota(jnp.int32, (ATTN_W, KEY_BLOCK), 0) & (LANES - 1)) < HEAD_DIM
    even_lane = (lax.broadcasted_iota(jnp.int32, (KEY_BLOCK, ATTN_W), 1) & (LANES - 1)) < HEAD_DIM
    for b in range(bb):
        for c in range(tm // blk):
            r0 = b * tm + c * blk
            kb, vb = k[r0:r0 + blk], v[r0:r0 + blk]
            if blk < KEY_BLOCK:
                fill = jnp.zeros((KEY_BLOCK - blk, ATTN_W), F32)
                kb, vb = jnp.concatenate([kb, fill], axis=0), jnp.concatenate([vb, fill], axis=0)
            kt = kb.T
            o = 2 * c * KEY_BLOCK
            kbt_ref[b, :, o:o + KEY_BLOCK] = jnp.where(even_row, kt, 0.0).astype(BF16)
            kbt_ref[b, :, o + KEY_BLOCK:o + 2 * KEY_BLOCK] = jnp.where(even_row, 0.0, kt).astype(BF16)
            vbd_ref[b, o:o + KEY_BLOCK, :] = jnp.where(even_lane, vb, 0.0).astype(BF16)
            vbd_ref[b, o + KEY_BLOCK:o + 2 * KEY_BLOCK, :] = jnp.where(even_lane, 0.0, vb).astype(BF16)

    c0 = 3 * ATTN_W
    u = proj(c0, CONV_CH) * _sigmoid(proj(c0 + CONV_CH, CONV_CH))
    u_ref[...] = u.reshape(bb, tm, CONV_CH)
    c0 += 2 * CONV_CH
    sga_ref[...] = _sigmoid(proj(c0, D_MODEL)).astype(BF16).reshape(bb, tm, D_MODEL)
    sgc_ref[...] = _sigmoid(proj(c0 + D_MODEL, D_MODEL)).astype(BF16).reshape(bb, tm, D_MODEL)


def _inproj(x, g1, win, b_in, qg, kg, hm, *, layer, bb, tm, kv_stack=None):
    nb, t, _ = x.shape
    grid = (nb // bb, t // tm)
    row = lambda width: pl.BlockSpec((bb, tm, width), lambda b, i: (b, i, 0))
    shp = lambda width, dt: jax.ShapeDtypeStruct((nb, t, width), dt)
    kw = 2 * KEY_BLOCK * (tm // min(tm, KEY_BLOCK))
    in_specs = [row(D_MODEL), _layer_spec((1, D_MODEL), layer), _layer_spec((D_MODEL, N_IN), layer),
                _layer_spec((1, N_IN), layer), _layer_spec((1, ATTN_W), layer),
                _layer_spec((1, ATTN_W), layer), _const_spec((ATTN_W, ATTN_W))]
    args = [x, g1, win, b_in, qg, kg, hm]
    aliases = {}
    if kv_stack is not None:
        kv_spec = pl.BlockSpec((None, bb, ATTN_W, tm), lambda b, i: (layer, b, 0, i))
        kv_shape = jax.ShapeDtypeStruct(kv_stack[0].shape, F32)
        aliases = {len(args): 1, len(args) + 1: 2}
        in_specs += [pl.BlockSpec(memory_space=pl.ANY)] * 2
        args += list(kv_stack)
    else:
        kv_spec, kv_shape = row(ATTN_W), shp(ATTN_W, F32)
    return pl.pallas_call(
        functools.partial(_inproj_kernel, feature_major=kv_stack is not None),
        grid=grid,
        in_specs=in_specs,
        input_output_aliases=aliases,
        out_specs=[row(ATTN_W), kv_spec, kv_spec,
                   pl.BlockSpec((bb, ATTN_W, kw), lambda b, i: (b, 0, i)),
                   pl.BlockSpec((bb, kw, ATTN_W), lambda b, i: (b, i, 0)),
                   row(CONV_CH), row(D_MODEL), row(D_MODEL)],
        out_shape=[shp(ATTN_W, BF16), kv_shape, kv_shape,
                   jax.ShapeDtypeStruct((nb, ATTN_W, kw * (t // tm)), BF16),
                   jax.ShapeDtypeStruct((nb, kw * (t // tm), ATTN_W), BF16),
                   shp(CONV_CH, F32), shp(D_MODEL, BF16), shp(D_MODEL, BF16)],
        compiler_params=pltpu.CompilerParams(
            dimension_semantics=("arbitrary", "arbitrary"), vmem_limit_bytes=VMEM_LIMIT),
        name="inproj",
    )(*args)


def _attn_kernel(*refs, tq, n_masked, n_past, layer):
    if n_past:
        (q_ref, kn_ref, vn_ref, um_ref, kp_ref, vp_ref, o_ref,
         carry_ref, acc_ref, sp_ref, ls_ref, kbuf_ref, vbuf_ref, sem_ref) = refs
    else:
        q_ref, kn_ref, vn_ref, um_ref, o_ref, carry_ref, acc_ref, sp_ref, ls_ref = refs
    batch = pl.program_id(0)
    i = pl.program_id(1)
    blocks_per_tile = tq // KEY_BLOCK
    carry_ref[...] = jnp.zeros_like(carry_ref)
    acc_ref[...] = jnp.zeros_like(acc_ref)

    def step(scores, weighted, key0=None, r0=0, slot=0):
        nr = tq - r0
        rows = slice(r0, tq)
        if key0 is None:
            mask = None
        else:
            qpos = i * tq + r0 + lax.broadcasted_iota(jnp.int32, (nr, 2 * KEY_BLOCK), 0)
            col = lax.broadcasted_iota(jnp.int32, (nr, 2 * KEY_BLOCK), 1) & (KEY_BLOCK - 1)
            mask = key0 + col < qpos
        for p in range(HEAD_PAIRS):
            z = scores(p, rows)
            sp = jnp.where(z > 64.0, z, jnp.log(1.0 + jnp.exp2(z)) * LOG2E)
            if mask is not None:
                sp = jnp.where(mask, sp, 0.0)
            sp_ref[slot, p * nr:(p + 1) * nr, :] = sp.astype(BF16)
            ls_ref[slot, p * nr:(p + 1) * nr, :] = z - sp
        later_all = jnp.dot(sp_ref[slot, 0:HEAD_PAIRS * nr, :], um_ref[...], preferred_element_type=F32)
        smallest = None
        for p in range(HEAD_PAIRS):
            later = later_all[p * nr:(p + 1) * nr]
            carry = carry_ref[p, rows, :]
            w = jnp.exp2(ls_ref[slot, p * nr:(p + 1) * nr, :] - later - carry)
            if mask is not None:
                w = jnp.where(mask, w, 0.0)
            acc_ref[p, rows, :] += weighted(p, w.astype(BF16))
            halves = []
            for c0 in (0, KEY_BLOCK):
                own = sp_ref[slot, p * nr:(p + 1) * nr, c0:c0 + KEY_BLOCK].astype(F32)
                halves.append(jnp.broadcast_to(later[:, c0:c0 + 1] + own[:, 0:1], (nr, KEY_BLOCK)))
            carry = carry + jnp.concatenate(halves, axis=1)
            carry_ref[p, rows, :] = carry
            smallest = carry if smallest is None else jnp.minimum(smallest, carry)
        return jnp.min(smallest)

    def new_block(jb):
        start = pl.multiple_of(jb * 2 * KEY_BLOCK, 2 * KEY_BLOCK)
        scores = lambda p, rows: jnp.dot(q_ref[0, rows, p * LANES:(p + 1) * LANES],
                                         kn_ref[0, p * LANES:(p + 1) * LANES, pl.ds(start, 2 * KEY_BLOCK)],
                                         preferred_element_type=F32)
        weighted = lambda p, w: jnp.dot(w, vn_ref[0, pl.ds(start, 2 * KEY_BLOCK), p * LANES:(p + 1) * LANES],
                                        preferred_element_type=F32)
        return scores, weighted

    def past_block(slot):
        even_feature = lax.broadcasted_iota(jnp.int32, (LANES, KEY_BLOCK), 0) < HEAD_DIM

        def split(buf_ref, p):
            blk = buf_ref[slot, p * LANES:(p + 1) * LANES, :]
            return jnp.concatenate([jnp.where(even_feature, blk, 0.0).astype(BF16),
                                    jnp.where(even_feature, 0.0, blk).astype(BF16)], axis=1)

        scores = lambda p, rows: jnp.dot(q_ref[0, rows, p * LANES:(p + 1) * LANES], split(kbuf_ref, p),
                                         preferred_element_type=F32)
        weighted = lambda p, w: lax.dot_general(w, split(vbuf_ref, p), (((1,), (1,)), ((), ())),
                                                preferred_element_type=F32)
        return scores, weighted

    def past_copies(n, slot):
        start = pl.multiple_of((n_past - 1 - n) * KEY_BLOCK, KEY_BLOCK)
        return [pltpu.make_async_copy(src.at[layer, batch, :, pl.ds(start, KEY_BLOCK)], dst.at[slot],
                                      sem_ref.at[which, slot])
                for which, (src, dst) in enumerate(((kp_ref, kbuf_ref), (vp_ref, vbuf_ref)))]

    smallest = jnp.float32(0.0)
    for d in range(n_masked):
        above = n_masked - 1 - d
        jb = i * blocks_per_tile + above
        smallest = step(*new_block(jb), key0=jb * KEY_BLOCK, r0=above * KEY_BLOCK, slot=d % 2)

    def unfinished(n_blocks):
        return lambda state: jnp.logical_and(state[0] < n_blocks, state[1] < CARRY_LIMIT)

    n_open = i * blocks_per_tile
    _, smallest = lax.while_loop(
        unfinished(n_open), lambda state: (state[0] + 1, step(*new_block(n_open - 1 - state[0]))),
        (jnp.int32(0), smallest))

    if n_past:
        for copy in past_copies(0, 0):
            copy.start()

        def past_body(state):
            n = state[0]
            slot = n & 1
            for copy in past_copies(n, slot):
                copy.wait()

            @pl.when(n + 1 < n_past)
            def _():
                for copy in past_copies(n + 1, 1 - slot):
                    copy.start()

            return n + 1, step(*past_block(slot))

        n_used, _ = lax.while_loop(unfinished(n_past), past_body, (jnp.int32(0), smallest))

        @pl.when(n_used < n_past)
        def _():
            for copy in past_copies(n_used, n_used & 1):
                copy.wait()

    for p in range(HEAD_PAIRS):
        o_ref[0, :, p * LANES:(p + 1) * LANES] = acc_ref[p].astype(BF16)


def _attention(q, kbt_new, vbd_new, um, *, tq, kt_cache=None, vt_cache=None, layer=0):
    nb, t, _ = q.shape
    tn2 = vbd_new.shape[1]
    n_masked = max(1, tq // KEY_BLOCK)
    n_past = 0 if kt_cache is None else kt_cache.shape[3] // KEY_BLOCK
    grid = (nb, t // tq)
    q_spec = pl.BlockSpec((1, tq, ATTN_W), lambda b, i: (b, i, 0))
    in_specs = [q_spec, pl.BlockSpec((1, ATTN_W, tn2), lambda b, i: (b, 0, 0)),
                pl.BlockSpec((1, tn2, ATTN_W), lambda b, i: (b, 0, 0)), _const_spec(um.shape)]
    args = [q, kbt_new, vbd_new, um]
    scratch = [pltpu.VMEM((HEAD_PAIRS, tq, 2 * KEY_BLOCK), F32),
               pltpu.VMEM((HEAD_PAIRS, tq, LANES), F32),
               pltpu.VMEM((2, HEAD_PAIRS * tq, 2 * KEY_BLOCK), BF16),
               pltpu.VMEM((2, HEAD_PAIRS * tq, 2 * KEY_BLOCK), F32)]
    if n_past:
        assert kt_cache.shape[3] % KEY_BLOCK == 0 and kt_cache.shape[1:3] == (nb, ATTN_W)
        in_specs += [pl.BlockSpec(memory_space=pl.ANY)] * 2
        args += [kt_cache, vt_cache]
        scratch += [pltpu.VMEM((2, ATTN_W, KEY_BLOCK), F32), pltpu.VMEM((2, ATTN_W, KEY_BLOCK), F32),
                    pltpu.SemaphoreType.DMA((2, 2))]
    return pl.pallas_call(
        functools.partial(_attn_kernel, tq=tq, n_masked=n_masked, n_past=n_past, layer=layer),
        grid=grid,
        in_specs=in_specs,
        out_specs=q_spec,
        out_shape=jax.ShapeDtypeStruct((nb, t, ATTN_W), BF16),
        scratch_shapes=scratch,
        compiler_params=pltpu.CompilerParams(
            dimension_semantics=("arbitrary",) * 2, vmem_limit_bytes=VMEM_LIMIT),
        name="attention",
    )(*args)


def _mix_kernel(x_ref, a_ref, u_ref, uh_ref, st_ref, sga_ref, sgc_ref, dww_ref, dwb_ref, cng_ref,
                cnb_ref, wco_ref, bco_ref, wao_ref, wo_ref, x1_ref, ext_ref, act_ref):
    i = pl.program_id(1)
    bb, tm, _ = x_ref.shape
    rows = bb * tm

    @pl.when(i == 0)
    def _():
        ext_ref[0, :, 0:CONV_HALO, :] = st_ref[...]

    @pl.when(i > 0)
    def _():
        ext_ref[0, :, 0:CONV_HALO, :] = uh_ref[...]

    ext_ref[0, :, CONV_HALO:CONV_HALO + tm, :] = u_ref[...]
    span = CONV_HALO + tm - SUBLANES
    for s in range(1, SUBLANES):
        ext_ref[s, :, 0:span, :] = ext_ref[0, :, s:s + span, :]

    chunk = 32
    first = CONV_HALO - (CONV_W - 1)

    def conv_rows(b, r0):
        acc = jnp.broadcast_to(dwb_ref[...], (chunk // SUBLANES, SUBLANES, CONV_CH))
        for j in range(CONV_W):
            s, base = (first + j) % SUBLANES, (first + j) // SUBLANES * SUBLANES
            taps = ext_ref[s, b, r0 + base:r0 + base + chunk, :]
            acc = acc + taps.reshape(chunk // SUBLANES, SUBLANES, CONV_CH) * dww_ref[j]
        acc = acc.reshape(chunk, CONV_CH)
        mu = jnp.mean(acc, axis=-1, keepdims=True)
        xc = acc - mu
        y = xc * lax.rsqrt(jnp.mean(xc * xc, axis=-1, keepdims=True) + EPS)
        y = y * cng_ref[...] + cnb_ref[...]
        act_ref[b * tm + r0:b * tm + r0 + chunk, :] = (y * _sigmoid(y)).astype(BF16)

    for b in range(bb):
        for r0 in range(0, tm, chunk):
            conv_rows(b, r0)
    c = jnp.dot(act_ref[...], wco_ref[...], preferred_element_type=F32) + bco_ref[...]
    a = jnp.dot(a_ref[...].reshape(rows, ATTN_W), wao_ref[...], preferred_element_type=F32)
    m = (sga_ref[...].reshape(rows, D_MODEL).astype(F32) * a
         + sgc_ref[...].reshape(rows, D_MODEL).astype(F32) * c)
    x1 = x_ref[...].reshape(rows, D_MODEL) + jnp.dot(m.astype(BF16), wo_ref[...],
                                                     preferred_element_type=F32)
    x1_ref[...] = x1.reshape(bb, tm, D_MODEL)


def _mix(x, a, u, state, sga, sgc, dww, dwb, cng, cnb, wco, bco, wao, wo, *, layer, bb, tm):
    nb, t, _ = x.shape
    grid = (nb // bb, t // tm)
    row = lambda width: pl.BlockSpec((bb, tm, width), lambda b, i: (b, i, 0))
    halo_blocks = tm // CONV_HALO
    halo = pl.BlockSpec((bb, CONV_HALO, CONV_CH),
                        lambda b, i: (b, jnp.maximum(i * halo_blocks - 1, 0), 0))
    st_spec = pl.BlockSpec((bb, CONV_HALO, CONV_CH), lambda b, i: (b, 0, 0))
    return pl.pallas_call(
        _mix_kernel,
        grid=grid,
        in_specs=[row(D_MODEL), row(ATTN_W), row(CONV_CH), halo, st_spec, row(D_MODEL), row(D_MODEL),
                  _layer_spec((CONV_W, SUBLANES, CONV_CH), layer), _layer_spec((1, CONV_CH), layer),
                  _layer_spec((1, CONV_CH), layer), _layer_spec((1, CONV_CH), layer),
                  _layer_spec((CONV_CH, D_MODEL), layer), _layer_spec((1, D_MODEL), layer),
                  _layer_spec((ATTN_W, D_MODEL), layer), _layer_spec((D_MODEL, D_MODEL), layer)],
        out_specs=row(D_MODEL),
        out_shape=jax.ShapeDtypeStruct((nb, t, D_MODEL), F32),
        scratch_shapes=[pltpu.VMEM((SUBLANES, bb, CONV_HALO + tm, CONV_CH), F32),
                        pltpu.VMEM((bb * tm, CONV_CH), BF16)],
        compiler_params=pltpu.CompilerParams(
            dimension_semantics=("arbitrary", "arbitrary"), vmem_limit_bytes=VMEM_LIMIT),
        name="mix",
    )(x, a, u, u, state, sga, sgc, dww, dwb, cng, cnb, wco, bco, wao, wo)


def _ffn_kernel(x1_ref, st_ref, g2_ref, wup_ref, fdw_ref, fdb_ref, wdn_ref, fg_ref,
                y_ref, nf_ref, prev_ref, extg_ref, extv_ref, act_ref, *, final_norm):
    i = pl.program_id(1)
    bb, tm, _ = x1_ref.shape
    rows = bb * tm

    @pl.when(i == 0)
    def _():
        prev_ref[...] = st_ref[...]

    x1 = x1_ref[...].reshape(rows, D_MODEL)
    h = x1 * lax.rsqrt(jnp.mean(x1 * x1, axis=-1, keepdims=True) + EPS) * g2_ref[...]
    h = h.astype(BF16)

    def conv(ext_ref, c0):
        up = jnp.dot(h, wup_ref[:, c0:c0 + FFN_CHUNK], preferred_element_type=F32)
        up = up.reshape(bb, tm, FFN_CHUNK)
        ext_ref[:, 0:FFN_HALO, :] = prev_ref[:, :, c0:c0 + FFN_CHUNK]
        ext_ref[:, FFN_HALO:FFN_HALO + tm, :] = up
        tail = up[:, tm - FFN_HALO:tm, :]
        prev_ref[:, :, c0:c0 + FFN_CHUNK] = tail
        nf_ref[:, :, c0:c0 + FFN_CHUNK] = tail
        w = fdw_ref[:, c0:c0 + FFN_CHUNK]
        out = (ext_ref[:, FFN_HALO - 2:FFN_HALO - 2 + tm, :] * w[0:1]
               + ext_ref[:, FFN_HALO - 1:FFN_HALO - 1 + tm, :] * w[1:2]
               + up * w[2:3] + fdb_ref[:, c0:c0 + FFN_CHUNK])
        return out.reshape(rows, FFN_CHUNK)

    for c in range(D_FF // FFN_CHUNK):
        gate = conv(extg_ref, c * FFN_CHUNK)
        val = conv(extv_ref, D_FF + c * FFN_CHUNK)
        act_ref[:, c * FFN_CHUNK:(c + 1) * FFN_CHUNK] = (gate * _sigmoid(gate) * val).astype(BF16)

    y = x1 + jnp.dot(act_ref[...], wdn_ref[...], preferred_element_type=F32)
    if final_norm:
        y = y * lax.rsqrt(jnp.mean(y * y, axis=-1, keepdims=True) + EPS) * fg_ref[...]
    y_ref[...] = y.reshape(bb, tm, D_MODEL)


def _ffn(x1, state, g2, wup, fdw, fdb, wdn, fg, *, layer, bb, tm, final_norm):
    nb, t, _ = x1.shape
    grid = (nb // bb, t // tm)
    row = pl.BlockSpec((bb, tm, D_MODEL), lambda b, i: (b, i, 0))
    st_spec = pl.BlockSpec((bb, FFN_HALO, 2 * D_FF), lambda b, i: (b, 0, 0))
    return pl.pallas_call(
        functools.partial(_ffn_kernel, final_norm=final_norm),
        grid=grid,
        in_specs=[row, st_spec, _layer_spec((1, D_MODEL), layer), _layer_spec((D_MODEL, 2 * D_FF), layer),
                  _layer_spec((FFN_CONV_W, 2 * D_FF), layer), _layer_spec((1, 2 * D_FF), layer),
                  _layer_spec((D_FF, D_MODEL), layer), _const_spec((1, D_MODEL))],
        out_specs=[row, st_spec],
        out_shape=[jax.ShapeDtypeStruct((nb, t, D_MODEL), F32),
                   jax.ShapeDtypeStruct((nb, FFN_HALO, 2 * D_FF), F32)],
        scratch_shapes=[pltpu.VMEM((bb, FFN_HALO, 2 * D_FF), F32),
                        pltpu.VMEM((bb, FFN_HALO + tm, FFN_CHUNK), F32),
                        pltpu.VMEM((bb, FFN_HALO + tm, FFN_CHUNK), F32),
                        pltpu.VMEM((bb * tm, D_FF), BF16)],
        compiler_params=pltpu.CompilerParams(
            dimension_semantics=("arbitrary", "arbitrary"), vmem_limit_bytes=VMEM_LIMIT),
        name="ffn",
    )(x1, state, g2, wup, fdw, fdb, wdn, fg)


def _cumsum_matrix():
    j = np.arange(2 * KEY_BLOCK)[:, None]
    s = np.arange(2 * KEY_BLOCK)[None, :]
    same = (j // KEY_BLOCK) == (s // KEY_BLOCK)
    return (same & ((j % KEY_BLOCK) > (s % KEY_BLOCK))).astype(np.float32)


def _head_mean_matrix():
    h = np.arange(ATTN_W) // HEAD_DIM
    return (h[:, None] == h[None, :]).astype(np.float32)


def _pad_rows_front(a, rows):
    return jnp.pad(a, ((0, 0), (rows - a.shape[1], 0), (0, 0)))


def kernel(x_prompt, x_sample, cache_sb_k, cache_sb_v, state_conv, state_ffn_conv, norm1_g, w_in, b_in,
           q_norm_g, k_norm_g, w_attn_out, dw_w, dw_b, cn_g, cn_b, w_conv_out, b_conv_out, w_out,
           norm2_g, w_up, ffn_dw_w, ffn_dw_b, w_down, final_g):
    depth = w_in.shape[0]
    bp, seq, _ = x_prompt.shape
    bs, dec, _ = x_sample.shape
    past = cache_sb_k.shape[2]
    um = jnp.asarray(_cumsum_matrix(), BF16)
    hm = jnp.asarray(_head_mean_matrix(), BF16)

    feature_major = lambda c: jnp.transpose(c, (0, 1, 3, 4, 2)).reshape(depth, bs, ATTN_W, past)
    kt_cache, vt_cache = feature_major(cache_sb_k), feature_major(cache_sb_v)
    conv_zero = jnp.zeros((bp, CONV_HALO, CONV_CH), F32)
    ffn_zero = jnp.zeros((bp, FFN_HALO, 2 * D_FF), F32)

    vec = lambda a: a.reshape(depth, 1, -1)
    per_head = lambda g: vec(jnp.tile(g, (1, N_HEADS)))
    proj_params = (vec(norm1_g), w_in.astype(BF16), vec(b_in), per_head(q_norm_g), per_head(k_norm_g))
    mix_params = (jnp.broadcast_to(dw_w[:, :, None, :], (depth, CONV_W, SUBLANES, CONV_CH)), vec(dw_b), vec(cn_g),
                  vec(cn_b), w_conv_out.astype(BF16), vec(b_conv_out), w_attn_out.astype(BF16), w_out.astype(BF16))
    ffn_params = (vec(norm2_g), w_up.astype(BF16), ffn_dw_w, vec(ffn_dw_b), w_down.astype(BF16))

    def run(l, x, conv_state, ffn_state, tiles, kv_stack=None):
        bb = tiles.batch
        q, k, v, kbt, vbd, u, sga, sgc = _inproj(x, *proj_params, hm, layer=l, bb=bb, tm=tiles.proj,
                                                 kv_stack=kv_stack)
        t = x.shape[1]
        if kv_stack is None:
            a = _attention(q, kbt, vbd, um, tq=tiles.attn, kt_cache=kt_cache, vt_cache=vt_cache, layer=l)
        else:
            a = _attention(q, kbt, vbd, um, tq=tiles.attn)
        x1 = _mix(x, a, u, conv_state, sga, sgc, *mix_params, layer=l, bb=bb, tm=tiles.mix)
        y, nf = _ffn(x1, ffn_state, *ffn_params, final_g.reshape(1, -1), layer=l, bb=bb, tm=tiles.ffn,
                     final_norm=l == depth - 1)
        return y, k, v, u[:, t - (CONV_W - 1):, :], nf[:, FFN_HALO - (FFN_CONV_W - 1):, :]

    xp, xs = x_prompt, x_sample
    kp, vp = jnp.zeros((depth, bp, ATTN_W, seq), F32), jnp.zeros((depth, bp, ATTN_W, seq), F32)
    outs = {name: [] for name in ("ks", "vs", "cp", "cs", "fp", "fs")}
    for l in range(depth):
        xp, kp, vp, c1, f1 = run(l, xp, conv_zero, ffn_zero, PROMPT_TILES, kv_stack=(kp, vp))
        xs, k2, v2, c2, f2 = run(l, xs, _pad_rows_front(state_conv[l], CONV_HALO),
                                 _pad_rows_front(state_ffn_conv[l], FFN_HALO),
                                 Tiles(batch=bs, proj=dec, attn=dec, mix=dec, ffn=dec))
        for name, val in zip(("ks", "vs", "cp", "cs", "fp", "fs"), (k2, v2, c1, c2, f1, f2)):
            outs[name].append(val)

    heads = lambda lst, b, t: jnp.stack(lst).reshape(depth, b, t, N_HEADS, HEAD_DIM)
    heads_t = lambda a: jnp.transpose(a.reshape(depth, bp, N_HEADS, HEAD_DIM, seq), (0, 1, 4, 2, 3))
    return (xp, xs, heads_t(kp), heads_t(vp), heads(outs["ks"], bs, dec), heads(outs["vs"], bs, dec),
            jnp.stack(outs["cp"]), jnp.stack(outs["cs"]), jnp.stack(outs["fp"]), jnp.stack(outs["fs"]))
```

```python
import functools
from typing import NamedTuple

import numpy as np
import jax
import jax.numpy as jnp
from jax import lax
from jax.experimental import pallas as pl
from jax.experimental.pallas import tpu as pltpu

D_MODEL = 1024
N_HEADS = 8
HEAD_DIM = 64
ATTN_W = N_HEADS * HEAD_DIM
CONV_CH = 512
CONV_W = 31
D_FF = 2816
FFN_CONV_W = 3
N_IN = 3 * ATTN_W + 2 * CONV_CH + 2 * D_MODEL
EPS = 1e-6

LANES = 128
SUBLANES = 8
KEY_BLOCK = 128
HEAD_PAIRS = ATTN_W // LANES
CONV_HALO = 32
FFN_HALO = SUBLANES
FFN_CHUNK = 256


class Tiles(NamedTuple):
    batch: int
    proj: int
    attn: int
    mix: int
    ffn: int


PROMPT_TILES = Tiles(batch=1, proj=512, attn=256, mix=256, ffn=256)
VMEM_LIMIT = 56 * 1024 * 1024
LOG2E = 1.4426950408889634
CARRY_LIMIT = 152.0
MASKED_LOG_WEIGHT = -1e30

F32 = jnp.float32
BF16 = jnp.bfloat16


def _const_spec(shape):
    return pl.BlockSpec(shape, lambda *_: (0,) * len(shape), pipeline_mode=pl.Buffered(1))


def _layer_spec(shape, layer):
    return pl.BlockSpec((None,) + tuple(shape), lambda *_: (layer,) + (0,) * len(shape),
                        pipeline_mode=pl.Buffered(1))


def _sigmoid(x):
    return 1.0 / (1.0 + jnp.exp(-x))


def _inproj_kernel(x_ref, g1_ref, win_ref, bin_ref, qg_ref, kg_ref, hm_ref, *refs, feature_major):
    q_ref, k_ref, v_ref, kbt_ref, vbd_ref, u_ref, sga_ref, sgc_ref = refs[2:] if feature_major else refs
    bb, tm, _ = x_ref.shape
    rows = bb * tm
    x = x_ref[...].reshape(rows, D_MODEL)
    h = x * lax.rsqrt(jnp.mean(x * x, axis=-1, keepdims=True) + EPS) * g1_ref[...]
    h = h.astype(BF16)

    def proj(c0, width):
        return (jnp.dot(h, win_ref[:, c0:c0 + width], preferred_element_type=F32)
                + bin_ref[:, c0:c0 + width])

    def head_rmsnorm(y, g):
        ssum = jnp.dot((y * y).astype(BF16), hm_ref[...], preferred_element_type=F32)
        return y * lax.rsqrt(ssum * (1.0 / HEAD_DIM) + EPS) * g

    q = head_rmsnorm(proj(0, ATTN_W), qg_ref[...])
    q_ref[...] = (q * (HEAD_DIM ** -0.5 * LOG2E)).astype(BF16).reshape(bb, tm, ATTN_W)
    k = head_rmsnorm(proj(ATTN_W, ATTN_W), kg_ref[...])
    v = proj(2 * ATTN_W, ATTN_W)
    if feature_major:
        for b in range(bb):
            k_ref[b] = k[b * tm:(b + 1) * tm].T
            v_ref[b] = v[b * tm:(b + 1) * tm].T
    else:
        k_ref[...] = k.reshape(bb, tm, ATTN_W)
        v_ref[...] = v.reshape(bb, tm, ATTN_W)

    blk = min(tm, KEY_BLOCK)
    even_row = (lax.broadcasted_iota(jnp.int32, (ATTN_W, KEY_BLOCK), 0) & (LANES - 1)) < HEAD_DIM
    even_lane = (lax.broadcasted_iota(jnp.int32, (KEY_BLOCK, ATTN_W), 1) & (LANES - 1)) < HEAD_DIM
    for b in range(bb):
        for c in range(tm // blk):
            r0 = b * tm + c * blk
            kb, vb = k[r0:r0 + blk], v[r0:r0 + blk]
            if blk < KEY_BLOCK:
                fill = jnp.zeros((KEY_BLOCK - blk, ATTN_W), F32)
                kb, vb = jnp.concatenate([kb, fill], axis=0), jnp.concatenate([vb, fill], axis=0)
            kt = kb.T
            o = 2 * c * KEY_BLOCK
            kbt_ref[b, :, o:o + KEY_BLOCK] = jnp.where(even_row, kt, 0.0).astype(BF16)
            kbt_ref[b, :, o + KEY_BLOCK:o + 2 * KEY_BLOCK] = jnp.where(even_row, 0.0, kt).astype(BF16)
            vbd_ref[b, o:o + KEY_BLOCK, :] = jnp.where(even_lane, vb, 0.0).astype(BF16)
            vbd_ref[b, o + KEY_BLOCK:o + 2 * KEY_BLOCK, :] = jnp.where(even_lane, 0.0, vb).astype(BF16)

    c0 = 3 * ATTN_W
    u = proj(c0, CONV_CH) * _sigmoid(proj(c0 + CONV_CH, CONV_CH))
    u_ref[...] = u.reshape(bb, tm, CONV_CH)
    c0 += 2 * CONV_CH
    sga_ref[...] = _sigmoid(proj(c0, D_MODEL)).astype(BF16).reshape(bb, tm, D_MODEL)
    sgc_ref[...] = _sigmoid(proj(c0 + D_MODEL, D_MODEL)).astype(BF16).reshape(bb, tm, D_MODEL)


def _inproj(x, g1, win, b_in, qg, kg, hm, *, layer, bb, tm, kv_stack=None):
    nb, t, _ = x.shape
    grid = (nb // bb, t // tm)
    row = lambda width: pl.BlockSpec((bb, tm, width), lambda b, i: (b, i, 0))
    shp = lambda width, dt: jax.ShapeDtypeStruct((nb, t, width), dt)
    kw = 2 * KEY_BLOCK * (tm // min(tm, KEY_BLOCK))
    in_specs = [row(D_MODEL), _layer_spec((1, D_MODEL), layer), _layer_spec((D_MODEL, N_IN), layer),
                _layer_spec((1, N_IN), layer), _layer_spec((1, ATTN_W), layer),
                _layer_spec((1, ATTN_W), layer), _const_spec((ATTN_W, ATTN_W))]
    args = [x, g1, win, b_in, qg, kg, hm]
    aliases = {}
    if kv_stack is not None:
        kv_spec = pl.BlockSpec((None, bb, ATTN_W, tm), lambda b, i: (layer, b, 0, i))
        kv_shape = jax.ShapeDtypeStruct(kv_stack[0].shape, F32)
        aliases = {len(args): 1, len(args) + 1: 2}
        in_specs += [pl.BlockSpec(memory_space=pl.ANY)] * 2
        args += list(kv_stack)
    else:
        kv_spec, kv_shape = row(ATTN_W), shp(ATTN_W, F32)
    return pl.pallas_call(
        functools.partial(_inproj_kernel, feature_major=kv_stack is not None),
        grid=grid,
        in_specs=in_specs,
        input_output_aliases=aliases,
        out_specs=[row(ATTN_W), kv_spec, kv_spec,
                   pl.BlockSpec((bb, ATTN_W, kw), lambda b, i: (b, 0, i)),
                   pl.BlockSpec((bb, kw, ATTN_W), lambda b, i: (b, i, 0)),
                   row(CONV_CH), row(D_MODEL), row(D_MODEL)],
        out_shape=[shp(ATTN_W, BF16), kv_shape, kv_shape,
                   jax.ShapeDtypeStruct((nb, ATTN_W, kw * (t // tm)), BF16),
                   jax.ShapeDtypeStruct((nb, kw * (t // tm), ATTN_W), BF16),
                   shp(CONV_CH, F32), shp(D_MODEL, BF16), shp(D_MODEL, BF16)],
        compiler_params=pltpu.CompilerParams(
            dimension_semantics=("arbitrary", "arbitrary"), vmem_limit_bytes=VMEM_LIMIT),
        name="inproj",
    )(*args)


def _attn_kernel(*refs, tq, n_past, layer):
    if n_past:
        (q_ref, kn_ref, vn_ref, um_ref, kp_ref, vp_ref, o_ref,
         carry_ref, acc_ref, sp_ref, zs_ref, kbuf_ref, vbuf_ref, sem_ref) = refs
    else:
        q_ref, kn_ref, vn_ref, um_ref, o_ref, carry_ref, acc_ref, sp_ref, zs_ref = refs
    batch = pl.program_id(0)
    i = pl.program_id(1)
    chains, sub = carry_ref.shape[0], carry_ref.shape[2]
    carry_ref[...] = jnp.zeros_like(carry_ref)
    acc_ref[...] = jnp.zeros_like(acc_ref)

    def step(c, blocks, key0=None):
        rows = slice(c * sub, (c + 1) * sub)
        if key0 is None:
            mask = None
        else:
            qpos = i * tq + c * sub + lax.broadcasted_iota(jnp.int32, (sub, 2 * KEY_BLOCK), 0)
            col = lax.broadcasted_iota(jnp.int32, (sub, 2 * KEY_BLOCK), 1) & (KEY_BLOCK - 1)
            mask = key0 + col < qpos
        part = lambda j, p: slice((j * HEAD_PAIRS + p) * sub, (j * HEAD_PAIRS + p + 1) * sub)
        for j, (scores, _) in enumerate(blocks):
            for p in range(HEAD_PAIRS):
                z = scores(p, rows)
                sp = jnp.where(z > 64.0, z, jnp.log(1.0 + jnp.exp2(z)) * LOG2E)
                if mask is not None:
                    sp = jnp.where(mask, sp, 0.0)
                sp_ref[c, part(j, p), :] = sp.astype(BF16)
                zs_ref[c, part(j, p), :] = z - sp
        later_all = jnp.dot(sp_ref[c, 0:len(blocks) * HEAD_PAIRS * sub, :], um_ref[...],
                            preferred_element_type=F32)
        for j, (_, weighted) in enumerate(blocks):
            smallest = None
            for p in range(HEAD_PAIRS):
                later = later_all[part(j, p)]
                carry = carry_ref[c, p]
                w = jnp.exp2(zs_ref[c, part(j, p), :] - later - carry)
                if mask is not None:
                    w = jnp.where(mask, w, 0.0)
                acc_ref[c, p] += weighted(p, w.astype(BF16))
                halves = []
                for c0 in (0, KEY_BLOCK):
                    own = sp_ref[c, part(j, p), c0:c0 + KEY_BLOCK].astype(F32)
                    halves.append(jnp.broadcast_to(later[:, c0:c0 + 1] + own[:, 0:1], (sub, KEY_BLOCK)))
                carry = carry + jnp.concatenate(halves, axis=1)
                carry_ref[c, p] = carry
                smallest = carry if smallest is None else jnp.minimum(smallest, carry)
        return jnp.min(smallest)

    def new_block(jb):
        start = pl.multiple_of(jb * 2 * KEY_BLOCK, 2 * KEY_BLOCK)
        scores = lambda p, rows: jnp.dot(q_ref[0, rows, p * LANES:(p + 1) * LANES],
                                         kn_ref[0, p * LANES:(p + 1) * LANES, pl.ds(start, 2 * KEY_BLOCK)],
                                         preferred_element_type=F32)
        weighted = lambda p, w: jnp.dot(w, vn_ref[0, pl.ds(start, 2 * KEY_BLOCK), p * LANES:(p + 1) * LANES],
                                        preferred_element_type=F32)
        return scores, weighted

    def past_block(slot):
        even_feature = lax.broadcasted_iota(jnp.int32, (LANES, KEY_BLOCK), 0) < HEAD_DIM

        def split(buf_ref, p):
            blk = buf_ref[slot, p * LANES:(p + 1) * LANES, :]
            return jnp.concatenate([jnp.where(even_feature, blk, 0.0).astype(BF16),
                                    jnp.where(even_feature, 0.0, blk).astype(BF16)], axis=1)

        scores = lambda p, rows: jnp.dot(q_ref[0, rows, p * LANES:(p + 1) * LANES], split(kbuf_ref, p),
                                         preferred_element_type=F32)
        weighted = lambda p, w: lax.dot_general(w, split(vbuf_ref, p), (((1,), (1,)), ((), ())),
                                                preferred_element_type=F32)
        return scores, weighted

    def past_copies(n, slot):
        start = pl.multiple_of((n_past - 1 - n) * KEY_BLOCK, KEY_BLOCK)
        return [pltpu.make_async_copy(src.at[layer, batch, :, pl.ds(start, KEY_BLOCK)], dst.at[slot],
                                      sem_ref.at[which, slot])
                for which, (src, dst) in enumerate(((kp_ref, kbuf_ref), (vp_ref, vbuf_ref)))]

    n_open = [(i * tq + c * sub) // KEY_BLOCK for c in range(chains)]

    def unfinished(n_blocks):
        return lambda state: jnp.logical_and(state[0] < n_blocks, state[1] < CARRY_LIMIT)

    smallest = [step(c, [new_block(n_open[c])], key0=n_open[c] * KEY_BLOCK) for c in range(chains)]
    if chains == 2:
        older = lambda c, k: [new_block(n_open[c] - 1 - k), new_block(n_open[c] - 2 - k)]
        k_both, _, last = lax.while_loop(
            lambda state: jnp.logical_and(state[0] + 2 <= n_open[0],
                                          jnp.minimum(state[1], state[2]) < CARRY_LIMIT),
            lambda state: (state[0] + 2, step(0, older(0, state[0])), step(1, older(1, state[0]))),
            (jnp.int32(0), smallest[0], smallest[1]))
        lax.while_loop(unfinished(n_open[1]),
                       lambda state: (state[0] + 1, step(1, [new_block(n_open[1] - 1 - state[0])])),
                       (k_both, last))
    else:
        _, smallest = lax.while_loop(
            unfinished(n_open[0]),
            lambda state: (state[0] + 1, step(0, [new_block(n_open[0] - 1 - state[0])])),
            (jnp.int32(0), smallest[0]))

    if n_past:
        for copy in past_copies(0, 0):
            copy.start()

        def past_body(state):
            n = state[0]
            slot = n & 1
            for copy in past_copies(n, slot):
                copy.wait()

            @pl.when(n + 1 < n_past)
            def _():
                for copy in past_copies(n + 1, 1 - slot):
                    copy.start()

            return n + 1, step(0, [past_block(slot)])

        n_used, _ = lax.while_loop(unfinished(n_past), past_body, (jnp.int32(0), smallest))

        @pl.when(n_used < n_past)
        def _():
            for copy in past_copies(n_used, n_used & 1):
                copy.wait()

    for c in range(chains):
        for p in range(HEAD_PAIRS):
            o_ref[0, c * sub:(c + 1) * sub, p * LANES:(p + 1) * LANES] = acc_ref[c, p].astype(BF16)


def _attention(q, kbt_new, vbd_new, um, *, tq, kt_cache=None, vt_cache=None, layer=0):
    nb, t, _ = q.shape
    tn2 = vbd_new.shape[1]
    n_past = 0 if kt_cache is None else kt_cache.shape[3] // KEY_BLOCK
    chains = max(1, tq // KEY_BLOCK)
    sub = tq // chains
    assert chains in (1, 2) and tq == chains * sub and (chains == 1 or sub == KEY_BLOCK)
    grid = (nb, t // tq)
    q_spec = pl.BlockSpec((1, tq, ATTN_W), lambda b, i: (b, i, 0))
    in_specs = [q_spec, pl.BlockSpec((1, ATTN_W, tn2), lambda b, i: (b, 0, 0)),
                pl.BlockSpec((1, tn2, ATTN_W), lambda b, i: (b, 0, 0)), _const_spec(um.shape)]
    args = [q, kbt_new, vbd_new, um]
    scratch = [pltpu.VMEM((chains, HEAD_PAIRS, sub, 2 * KEY_BLOCK), F32),
               pltpu.VMEM((chains, HEAD_PAIRS, sub, LANES), F32),
               pltpu.VMEM((chains, 2 * HEAD_PAIRS * sub, 2 * KEY_BLOCK), BF16),
               pltpu.VMEM((chains, 2 * HEAD_PAIRS * sub, 2 * KEY_BLOCK), F32)]
    if n_past:
        assert chains == 1 and kt_cache.shape[3] % KEY_BLOCK == 0 and kt_cache.shape[1:3] == (nb, ATTN_W)
        in_specs += [pl.BlockSpec(memory_space=pl.ANY)] * 2
        args += [kt_cache, vt_cache]
        scratch += [pltpu.VMEM((2, ATTN_W, KEY_BLOCK), F32), pltpu.VMEM((2, ATTN_W, KEY_BLOCK), F32),
                    pltpu.SemaphoreType.DMA((2, 2))]
    return pl.pallas_call(
        functools.partial(_attn_kernel, tq=tq, n_past=n_past, layer=layer),
        grid=grid,
        in_specs=in_specs,
        out_specs=q_spec,
        out_shape=jax.ShapeDtypeStruct((nb, t, ATTN_W), BF16),
        scratch_shapes=scratch,
        compiler_params=pltpu.CompilerParams(
            dimension_semantics=("arbitrary",) * 2, vmem_limit_bytes=VMEM_LIMIT),
        name="attention",
    )(*args)


def _mix_kernel(x_ref, a_ref, u_ref, uh_ref, st_ref, sga_ref, sgc_ref, dww_ref, dwb_ref, cng_ref,
                cnb_ref, wco_ref, bco_ref, wao_ref, wo_ref, x1_ref, ext_ref, act_ref):
    i = pl.program_id(1)
    bb, tm, _ = x_ref.shape
    rows = bb * tm

    @pl.when(i == 0)
    def _():
        ext_ref[0, :, 0:CONV_HALO, :] = st_ref[...]

    @pl.when(i > 0)
    def _():
        ext_ref[0, :, 0:CONV_HALO, :] = uh_ref[...]

    ext_ref[0, :, CONV_HALO:CONV_HALO + tm, :] = u_ref[...]
    span = CONV_HALO + tm - SUBLANES
    for s in range(1, SUBLANES):
        ext_ref[s, :, 0:span, :] = ext_ref[0, :, s:s + span, :]

    chunk = 32
    first = CONV_HALO - (CONV_W - 1)

    def conv_rows(b, r0):
        acc = jnp.broadcast_to(dwb_ref[...], (chunk // SUBLANES, SUBLANES, CONV_CH))
        for j in range(CONV_W):
            s, base = (first + j) % SUBLANES, (first + j) // SUBLANES * SUBLANES
            taps = ext_ref[s, b, r0 + base:r0 + base + chunk, :]
            acc = acc + taps.reshape(chunk // SUBLANES, SUBLANES, CONV_CH) * dww_ref[j]
        acc = acc.reshape(chunk, CONV_CH)
        mu = jnp.mean(acc, axis=-1, keepdims=True)
        xc = acc - mu
        y = xc * lax.rsqrt(jnp.mean(xc * xc, axis=-1, keepdims=True) + EPS)
        y = y * cng_ref[...] + cnb_ref[...]
        act_ref[b * tm + r0:b * tm + r0 + chunk, :] = (y * _sigmoid(y)).astype(BF16)

    for b in range(bb):
        for r0 in range(0, tm, chunk):
            conv_rows(b, r0)
    c = jnp.dot(act_ref[...], wco_ref[...], preferred_element_type=F32) + bco_ref[...]
    a = jnp.dot(a_ref[...].reshape(rows, ATTN_W), wao_ref[...], preferred_element_type=F32)
    m = (sga_ref[...].reshape(rows, D_MODEL).astype(F32) * a
         + sgc_ref[...].reshape(rows, D_MODEL).astype(F32) * c)
    x1 = x_ref[...].reshape(rows, D_MODEL) + jnp.dot(m.astype(BF16), wo_ref[...],
                                                     preferred_element_type=F32)
    x1_ref[...] = x1.reshape(bb, tm, D_MODEL)


def _mix(x, a, u, state, sga, sgc, dww, dwb, cng, cnb, wco, bco, wao, wo, *, layer, bb, tm):
    nb, t, _ = x.shape
    grid = (nb // bb, t // tm)
    row = lambda width: pl.BlockSpec((bb, tm, width), lambda b, i: (b, i, 0))
    halo_blocks = tm // CONV_HALO
    halo = pl.BlockSpec((bb, CONV_HALO, CONV_CH),
                        lambda b, i: (b, jnp.maximum(i * halo_blocks - 1, 0), 0))
    st_spec = pl.BlockSpec((bb, CONV_HALO, CONV_CH), lambda b, i: (b, 0, 0))
    return pl.pallas_call(
        _mix_kernel,
        grid=grid,
        in_specs=[row(D_MODEL), row(ATTN_W), row(CONV_CH), halo, st_spec, row(D_MODEL), row(D_MODEL),
                  _layer_spec((CONV_W, SUBLANES, CONV_CH), layer), _layer_spec((1, CONV_CH), layer),
                  _layer_spec((1, CONV_CH), layer), _layer_spec((1, CONV_CH), layer),
                  _layer_spec((CONV_CH, D_MODEL), layer), _layer_spec((1, D_MODEL), layer),
                  _layer_spec((ATTN_W, D_MODEL), layer), _layer_spec((D_MODEL, D_MODEL), layer)],
        out_specs=row(D_MODEL),
        out_shape=jax.ShapeDtypeStruct((nb, t, D_MODEL), F32),
        scratch_shapes=[pltpu.VMEM((SUBLANES, bb, CONV_HALO + tm, CONV_CH), F32),
                        pltpu.VMEM((bb * tm, CONV_CH), BF16)],
        compiler_params=pltpu.CompilerParams(
            dimension_semantics=("arbitrary", "arbitrary"), vmem_limit_bytes=VMEM_LIMIT),
        name="mix",
    )(x, a, u, u, state, sga, sgc, dww, dwb, cng, cnb, wco, bco, wao, wo)


def _ffn_kernel(x1_ref, st_ref, g2_ref, wup_ref, fdw_ref, fdb_ref, wdn_ref, fg_ref,
                y_ref, nf_ref, prev_ref, extg_ref, extv_ref, act_ref, *, final_norm):
    i = pl.program_id(1)
    bb, tm, _ = x1_ref.shape
    rows = bb * tm

    @pl.when(i == 0)
    def _():
        prev_ref[...] = st_ref[...]

    x1 = x1_ref[...].reshape(rows, D_MODEL)
    h = x1 * lax.rsqrt(jnp.mean(x1 * x1, axis=-1, keepdims=True) + EPS) * g2_ref[...]
    h = h.astype(BF16)

    def conv(ext_ref, c0):
        up = jnp.dot(h, wup_ref[:, c0:c0 + FFN_CHUNK], preferred_element_type=F32)
        up = up.reshape(bb, tm, FFN_CHUNK)
        ext_ref[:, 0:FFN_HALO, :] = prev_ref[:, :, c0:c0 + FFN_CHUNK]
        ext_ref[:, FFN_HALO:FFN_HALO + tm, :] = up
        tail = up[:, tm - FFN_HALO:tm, :]
        prev_ref[:, :, c0:c0 + FFN_CHUNK] = tail
        nf_ref[:, :, c0:c0 + FFN_CHUNK] = tail
        w = fdw_ref[:, c0:c0 + FFN_CHUNK]
        out = (ext_ref[:, FFN_HALO - 2:FFN_HALO - 2 + tm, :] * w[0:1]
               + ext_ref[:, FFN_HALO - 1:FFN_HALO - 1 + tm, :] * w[1:2]
               + up * w[2:3] + fdb_ref[:, c0:c0 + FFN_CHUNK])
        return out.reshape(rows, FFN_CHUNK)

    for c in range(D_FF // FFN_CHUNK):
        gate = conv(extg_ref, c * FFN_CHUNK)
        val = conv(extv_ref, D_FF + c * FFN_CHUNK)
        act_ref[:, c * FFN_CHUNK:(c + 1) * FFN_CHUNK] = (gate * _sigmoid(gate) * val).astype(BF16)

    y = x1 + jnp.dot(act_ref[...], wdn_ref[...], preferred_element_type=F32)
    if final_norm:
        y = y * lax.rsqrt(jnp.mean(y * y, axis=-1, keepdims=True) + EPS) * fg_ref[...]
    y_ref[...] = y.reshape(bb, tm, D_MODEL)


def _ffn(x1, state, g2, wup, fdw, fdb, wdn, fg, *, layer, bb, tm, final_norm):
    nb, t, _ = x1.shape
    grid = (nb // bb, t // tm)
    row = pl.BlockSpec((bb, tm, D_MODEL), lambda b, i: (b, i, 0))
    st_spec = pl.BlockSpec((bb, FFN_HALO, 2 * D_FF), lambda b, i: (b, 0, 0))
    return pl.pallas_call(
        functools.partial(_ffn_kernel, final_norm=final_norm),
        grid=grid,
        in_specs=[row, st_spec, _layer_spec((1, D_MODEL), layer), _layer_spec((D_MODEL, 2 * D_FF), layer),
                  _layer_spec((FFN_CONV_W, 2 * D_FF), layer), _layer_spec((1, 2 * D_FF), layer),
                  _layer_spec((D_FF, D_MODEL), layer), _const_spec((1, D_MODEL))],
        out_specs=[row, st_spec],
        out_shape=[jax.ShapeDtypeStruct((nb, t, D_MODEL), F32),
                   jax.ShapeDtypeStruct((nb, FFN_HALO, 2 * D_FF), F32)],
        scratch_shapes=[pltpu.VMEM((bb, FFN_HALO, 2 * D_FF), F32),
                        pltpu.VMEM((bb, FFN_HALO + tm, FFN_CHUNK), F32),
                        pltpu.VMEM((bb, FFN_HALO + tm, FFN_CHUNK), F32),
                        pltpu.VMEM((bb * tm, D_FF), BF16)],
        compiler_params=pltpu.CompilerParams(
            dimension_semantics=("arbitrary", "arbitrary"), vmem_limit_bytes=VMEM_LIMIT),
        name="ffn",
    )(x1, state, g2, wup, fdw, fdb, wdn, fg)


def _cumsum_matrix():
    j = np.arange(2 * KEY_BLOCK)[:, None]
    s = np.arange(2 * KEY_BLOCK)[None, :]
    same = (j // KEY_BLOCK) == (s // KEY_BLOCK)
    return (same & ((j % KEY_BLOCK) > (s % KEY_BLOCK))).astype(np.float32)


def _head_mean_matrix():
    h = np.arange(ATTN_W) // HEAD_DIM
    return (h[:, None] == h[None, :]).astype(np.float32)


def _pad_rows_front(a, rows):
    return jnp.pad(a, ((0, 0), (rows - a.shape[1], 0), (0, 0)))


def kernel(x_prompt, x_sample, cache_sb_k, cache_sb_v, state_conv, state_ffn_conv, norm1_g, w_in, b_in,
           q_norm_g, k_norm_g, w_attn_out, dw_w, dw_b, cn_g, cn_b, w_conv_out, b_conv_out, w_out,
           norm2_g, w_up, ffn_dw_w, ffn_dw_b, w_down, final_g):
    depth = w_in.shape[0]
    bp, seq, _ = x_prompt.shape
    bs, dec, _ = x_sample.shape
    past = cache_sb_k.shape[2]
    um = jnp.asarray(_cumsum_matrix(), BF16)
    hm = jnp.asarray(_head_mean_matrix(), BF16)

    feature_major = lambda c: jnp.transpose(c, (0, 1, 3, 4, 2)).reshape(depth, bs, ATTN_W, past)
    kt_cache, vt_cache = feature_major(cache_sb_k), feature_major(cache_sb_v)
    conv_zero = jnp.zeros((bp, CONV_HALO, CONV_CH), F32)
    ffn_zero = jnp.zeros((bp, FFN_HALO, 2 * D_FF), F32)

    vec = lambda a: a.reshape(depth, 1, -1)
    per_head = lambda g: vec(jnp.tile(g, (1, N_HEADS)))
    proj_params = (vec(norm1_g), w_in.astype(BF16), vec(b_in), per_head(q_norm_g), per_head(k_norm_g))
    mix_params = (jnp.broadcast_to(dw_w[:, :, None, :], (depth, CONV_W, SUBLANES, CONV_CH)), vec(dw_b), vec(cn_g),
                  vec(cn_b), w_conv_out.astype(BF16), vec(b_conv_out), w_attn_out.astype(BF16), w_out.astype(BF16))
    ffn_params = (vec(norm2_g), w_up.astype(BF16), ffn_dw_w, vec(ffn_dw_b), w_down.astype(BF16))

    def run(l, x, conv_state, ffn_state, tiles, kv_stack=None):
        bb = tiles.batch
        q, k, v, kbt, vbd, u, sga, sgc = _inproj(x, *proj_params, hm, layer=l, bb=bb, tm=tiles.proj,
                                                 kv_stack=kv_stack)
        t = x.shape[1]
        if kv_stack is None:
            a = _attention(q, kbt, vbd, um, tq=tiles.attn, kt_cache=kt_cache, vt_cache=vt_cache, layer=l)
        else:
            a = _attention(q, kbt, vbd, um, tq=tiles.attn)
        x1 = _mix(x, a, u, conv_state, sga, sgc, *mix_params, layer=l, bb=bb, tm=tiles.mix)
        y, nf = _ffn(x1, ffn_state, *ffn_params, final_g.reshape(1, -1), layer=l, bb=bb, tm=tiles.ffn,
                     final_norm=l == depth - 1)
        return y, k, v, u[:, t - (CONV_W - 1):, :], nf[:, FFN_HALO - (FFN_CONV_W - 1):, :]

    xp, xs = x_prompt, x_sample
    kp, vp = jnp.zeros((depth, bp, ATTN_W, seq), F32), jnp.zeros((depth, bp, ATTN_W, seq), F32)
    outs = {name: [] for name in ("ks", "vs", "cp", "cs", "fp", "fs")}
    for l in range(depth):
        xp, kp, vp, c1, f1 = run(l, xp, conv_zero, ffn_zero, PROMPT_TILES, kv_stack=(kp, vp))
        xs, k2, v2, c2, f2 = run(l, xs, _pad_rows_front(state_conv[l], CONV_HALO),
                                 _pad_rows_front(state_ffn_conv[l], FFN_HALO),
                                 Tiles(batch=bs, proj=dec, attn=dec, mix=dec, ffn=dec))
        for name, val in zip(("ks", "vs", "cp", "cs", "fp", "fs"), (k2, v2, c1, c2, f1, f2)):
            outs[name].append(val)

    heads = lambda lst, b, t: jnp.stack(lst).reshape(depth, b, t, N_HEADS, HEAD_DIM)
    heads_t = lambda a: jnp.transpose(a.reshape(depth, bp, N_HEADS, HEAD_DIM, seq), (0, 1, 4, 2, 3))
    return (xp, xs, heads_t(kp), heads_t(vp), heads(outs["ks"], bs, dec), heads(outs["vs"], bs, dec),
            jnp.stack(outs["cp"]), jnp.stack(outs["cs"]), jnp.stack(outs["fp"]), jnp.stack(outs["fs"]))
```

```python
import functools
from typing import NamedTuple

import numpy as np
import jax
import jax.numpy as jnp
from jax import lax
from jax.experimental import pallas as pl
from jax.experimental.pallas import tpu as pltpu

D_MODEL = 1024
N_HEADS = 8
HEAD_DIM = 64
ATTN_W = N_HEADS * HEAD_DIM
CONV_CH = 512
CONV_W = 31
D_FF = 2816
FFN_CONV_W = 3
N_IN = 3 * ATTN_W + 2 * CONV_CH + 2 * D_MODEL
EPS = 1e-6

LANES = 128
SUBLANES = 8
MXU_WIDTH = 256
KEY_BLOCK = 128
HEAD_PAIRS = ATTN_W // LANES
CONV_HALO = 32
FFN_HALO = SUBLANES
FFN_CHUNK = 256


class Tiles(NamedTuple):
    batch: int
    proj: int
    attn: int
    mix: int
    ffn: int


PROMPT_TILES = Tiles(batch=1, proj=512, attn=256, mix=256, ffn=256)
VMEM_LIMIT = 56 * 1024 * 1024
LOG2E = 1.4426950408889634
CARRY_LIMIT = 152.0
MASKED_LOG_WEIGHT = -1e30

F32 = jnp.float32
BF16 = jnp.bfloat16


def _const_spec(shape):
    return pl.BlockSpec(shape, lambda *_: (0,) * len(shape), pipeline_mode=pl.Buffered(1))


def _compiler_params():
    return pltpu.CompilerParams(dimension_semantics=("arbitrary", "arbitrary"), vmem_limit_bytes=VMEM_LIMIT)


def _layer_spec(shape, layer):
    return pl.BlockSpec((None,) + tuple(shape), lambda *_: (layer,) + (0,) * len(shape),
                        pipeline_mode=pl.Buffered(1))


def _sigmoid(x):
    return 1.0 / (1.0 + jnp.exp(-x))


def _inproj_kernel(x_ref, g1_ref, win_ref, bin_ref, qg_ref, kg_ref, hm_ref, *refs, feature_major):
    q_ref, k_ref, v_ref, kbt_ref, vbd_ref, u_ref, sga_ref, sgc_ref = refs[-8:]
    bb, tm, _ = x_ref.shape
    rows = bb * tm
    x = x_ref[...].reshape(rows, D_MODEL)
    h = x * lax.rsqrt(jnp.mean(x * x, axis=-1, keepdims=True) + EPS) * g1_ref[...]
    h = h.astype(BF16)

    def proj(c0, width):
        return (jnp.dot(h, win_ref[:, c0:c0 + width], preferred_element_type=F32)
                + bin_ref[:, c0:c0 + width])

    def head_rmsnorm(y, g):
        ssum = jnp.dot((y * y).astype(BF16), hm_ref[...], preferred_element_type=F32)
        return y * lax.rsqrt(ssum * (1.0 / HEAD_DIM) + EPS) * g

    q = head_rmsnorm(proj(0, ATTN_W), qg_ref[...])
    q_ref[...] = (q * (HEAD_DIM ** -0.5 * LOG2E)).astype(BF16).reshape(bb, tm, ATTN_W)
    k = head_rmsnorm(proj(ATTN_W, ATTN_W), kg_ref[...])
    v = proj(2 * ATTN_W, ATTN_W)
    if feature_major:
        for b in range(bb):
            k_ref[b] = k[b * tm:(b + 1) * tm].T
            v_ref[b] = v[b * tm:(b + 1) * tm].T
    else:
        k_ref[...] = k.reshape(bb, tm, ATTN_W)
        v_ref[...] = v.reshape(bb, tm, ATTN_W)

    blk = min(tm, KEY_BLOCK)
    even_row = (lax.broadcasted_iota(jnp.int32, (ATTN_W, KEY_BLOCK), 0) & (LANES - 1)) < HEAD_DIM
    even_lane = (lax.broadcasted_iota(jnp.int32, (KEY_BLOCK, ATTN_W), 1) & (LANES - 1)) < HEAD_DIM
    for b in range(bb):
        for c in range(tm // blk):
            r0 = b * tm + c * blk
            kb, vb = k[r0:r0 + blk], v[r0:r0 + blk]
            if blk < KEY_BLOCK:
                fill = jnp.zeros((KEY_BLOCK - blk, ATTN_W), F32)
                kb, vb = jnp.concatenate([kb, fill], axis=0), jnp.concatenate([vb, fill], axis=0)
            kt = kb.T
            o = 2 * c * KEY_BLOCK
            kbt_ref[b, :, o:o + KEY_BLOCK] = jnp.where(even_row, kt, 0.0).astype(BF16)
            kbt_ref[b, :, o + KEY_BLOCK:o + 2 * KEY_BLOCK] = jnp.where(even_row, 0.0, kt).astype(BF16)
            vbd_ref[b, o:o + KEY_BLOCK, :] = jnp.where(even_lane, vb, 0.0).astype(BF16)
            vbd_ref[b, o + KEY_BLOCK:o + 2 * KEY_BLOCK, :] = jnp.where(even_lane, 0.0, vb).astype(BF16)

    c0 = 3 * ATTN_W
    u = proj(c0, CONV_CH) * _sigmoid(proj(c0 + CONV_CH, CONV_CH))
    u_ref[...] = u.reshape(bb, tm, CONV_CH)
    c0 += 2 * CONV_CH
    sga_ref[...] = _sigmoid(proj(c0, D_MODEL)).astype(BF16).reshape(bb, tm, D_MODEL)
    sgc_ref[...] = _sigmoid(proj(c0 + D_MODEL, D_MODEL)).astype(BF16).reshape(bb, tm, D_MODEL)


def _inproj(x, g1, win, b_in, qg, kg, hm, *, layer, bb, tm, kv_stack=None):
    nb, t, _ = x.shape
    grid = (nb // bb, t // tm)
    row = lambda width: pl.BlockSpec((bb, tm, width), lambda b, i: (b, i, 0))
    shp = lambda width, dt: jax.ShapeDtypeStruct((nb, t, width), dt)
    kw = 2 * KEY_BLOCK * (tm // min(tm, KEY_BLOCK))
    in_specs = [row(D_MODEL), _layer_spec((1, D_MODEL), layer), _layer_spec((D_MODEL, N_IN), layer),
                _layer_spec((1, N_IN), layer), _layer_spec((1, ATTN_W), layer),
                _layer_spec((1, ATTN_W), layer), _const_spec((ATTN_W, ATTN_W))]
    args = [x, g1, win, b_in, qg, kg, hm]
    aliases = {}
    if kv_stack is not None:
        kv_spec = pl.BlockSpec((None, bb, ATTN_W, tm), lambda b, i: (layer, b, 0, i))
        kv_shape = jax.ShapeDtypeStruct(kv_stack[0].shape, F32)
        aliases = {len(args): 1, len(args) + 1: 2}
        in_specs += [pl.BlockSpec(memory_space=pl.ANY)] * 2
        args += list(kv_stack)
    else:
        kv_spec, kv_shape = row(ATTN_W), shp(ATTN_W, F32)
    return pl.pallas_call(
        functools.partial(_inproj_kernel, feature_major=kv_stack is not None),
        grid=grid,
        in_specs=in_specs,
        input_output_aliases=aliases,
        out_specs=[row(ATTN_W), kv_spec, kv_spec,
                   pl.BlockSpec((bb, ATTN_W, kw), lambda b, i: (b, 0, i)),
                   pl.BlockSpec((bb, kw, ATTN_W), lambda b, i: (b, i, 0)),
                   row(CONV_CH), row(D_MODEL), row(D_MODEL)],
        out_shape=[shp(ATTN_W, BF16), kv_shape, kv_shape,
                   jax.ShapeDtypeStruct((nb, ATTN_W, kw * (t // tm)), BF16),
                   jax.ShapeDtypeStruct((nb, kw * (t // tm), ATTN_W), BF16),
                   shp(CONV_CH, F32), shp(D_MODEL, BF16), shp(D_MODEL, BF16)],
        compiler_params=_compiler_params(),
        name="inproj",
    )(*args)


def _attn_kernel(*refs, tq, n_past, layer):
    if n_past:
        (q_ref, kn_ref, vn_ref, um_ref, kp_ref, vp_ref, o_ref,
         carry_ref, acc_ref, sp_ref, zs_ref, kbuf_ref, vbuf_ref, sem_ref) = refs
    else:
        q_ref, kn_ref, vn_ref, um_ref, o_ref, carry_ref, acc_ref, sp_ref, zs_ref = refs
    batch = pl.program_id(0)
    i = pl.program_id(1)
    chains, sub = carry_ref.shape[0], carry_ref.shape[2]
    carry_ref[...] = jnp.zeros_like(carry_ref)
    acc_ref[...] = jnp.zeros_like(acc_ref)

    def step(c, blocks, key0=None):
        rows = slice(c * sub, (c + 1) * sub)
        if key0 is None:
            mask = None
        else:
            qpos = i * tq + c * sub + lax.broadcasted_iota(jnp.int32, (sub, 2 * KEY_BLOCK), 0)
            col = lax.broadcasted_iota(jnp.int32, (sub, 2 * KEY_BLOCK), 1) & (KEY_BLOCK - 1)
            mask = key0 + col < qpos
        part = lambda j, p: slice((j * HEAD_PAIRS + p) * sub, (j * HEAD_PAIRS + p + 1) * sub)
        for j, (scores, _) in enumerate(blocks):
            for p in range(HEAD_PAIRS):
                z = scores(p, rows)
                sp = jnp.where(z > 64.0, z, jnp.log(1.0 + jnp.exp2(z)) * LOG2E)
                if mask is not None:
                    sp = jnp.where(mask, sp, 0.0)
                sp_ref[c, part(j, p), :] = sp.astype(BF16)
                zs_ref[c, part(j, p), :] = z - sp
        later_all = jnp.dot(sp_ref[c, 0:len(blocks) * HEAD_PAIRS * sub, :], um_ref[...],
                            preferred_element_type=F32)
        for j, (_, weighted) in enumerate(blocks):
            smallest = None
            for p in range(HEAD_PAIRS):
                later = later_all[part(j, p)]
                carry = carry_ref[c, p]
                w = jnp.exp2(zs_ref[c, part(j, p), :] - later - carry)
                if mask is not None:
                    w = jnp.where(mask, w, 0.0)
                acc_ref[c, p] += weighted(p, w.astype(BF16))
                halves = []
                for c0 in (0, KEY_BLOCK):
                    own = sp_ref[c, part(j, p), c0:c0 + KEY_BLOCK].astype(F32)
                    halves.append(jnp.broadcast_to(later[:, c0:c0 + 1] + own[:, 0:1], (sub, KEY_BLOCK)))
                carry = carry + jnp.concatenate(halves, axis=1)
                carry_ref[c, p] = carry
                smallest = carry if smallest is None else jnp.minimum(smallest, carry)
        return jnp.min(smallest)

    def new_block(jb):
        start = pl.multiple_of(jb * 2 * KEY_BLOCK, 2 * KEY_BLOCK)
        scores = lambda p, rows: jnp.dot(q_ref[0, rows, p * LANES:(p + 1) * LANES],
                                         kn_ref[0, p * LANES:(p + 1) * LANES, pl.ds(start, 2 * KEY_BLOCK)],
                                         preferred_element_type=F32)
        weighted = lambda p, w: jnp.dot(w, vn_ref[0, pl.ds(start, 2 * KEY_BLOCK), p * LANES:(p + 1) * LANES],
                                        preferred_element_type=F32)
        return scores, weighted

    def past_block(slot):
        even_feature = lax.broadcasted_iota(jnp.int32, (LANES, KEY_BLOCK), 0) < HEAD_DIM

        def split(buf_ref, p):
            blk = buf_ref[slot, p * LANES:(p + 1) * LANES, :]
            return jnp.concatenate([jnp.where(even_feature, blk, 0.0).astype(BF16),
                                    jnp.where(even_feature, 0.0, blk).astype(BF16)], axis=1)

        scores = lambda p, rows: jnp.dot(q_ref[0, rows, p * LANES:(p + 1) * LANES], split(kbuf_ref, p),
                                         preferred_element_type=F32)
        weighted = lambda p, w: lax.dot_general(w, split(vbuf_ref, p), (((1,), (1,)), ((), ())),
                                                preferred_element_type=F32)
        return scores, weighted

    def past_copies(n, slot):
        start = pl.multiple_of((n_past - 1 - n) * KEY_BLOCK, KEY_BLOCK)
        return [pltpu.make_async_copy(src.at[layer, batch, :, pl.ds(start, KEY_BLOCK)], dst.at[slot],
                                      sem_ref.at[which, slot])
                for which, (src, dst) in enumerate(((kp_ref, kbuf_ref), (vp_ref, vbuf_ref)))]

    n_open = [(i * tq + c * sub) // KEY_BLOCK for c in range(chains)]

    def unfinished(n_blocks):
        return lambda state: jnp.logical_and(state[0] < n_blocks, state[1] < CARRY_LIMIT)

    smallest = [step(c, [new_block(n_open[c])], key0=n_open[c] * KEY_BLOCK) for c in range(chains)]
    if chains == 2:
        older = lambda c, k: [new_block(n_open[c] - 1 - k), new_block(n_open[c] - 2 - k)]
        k_both, _, last = lax.while_loop(
            lambda state: jnp.logical_and(state[0] + 2 <= n_open[0],
                                          jnp.minimum(state[1], state[2]) < CARRY_LIMIT),
            lambda state: (state[0] + 2, step(0, older(0, state[0])), step(1, older(1, state[0]))),
            (jnp.int32(0), smallest[0], smallest[1]))
        lax.while_loop(unfinished(n_open[1]),
                       lambda state: (state[0] + 1, step(1, [new_block(n_open[1] - 1 - state[0])])),
                       (k_both, last))
    else:
        _, smallest = lax.while_loop(
            unfinished(n_open[0]),
            lambda state: (state[0] + 1, step(0, [new_block(n_open[0] - 1 - state[0])])),
            (jnp.int32(0), smallest[0]))

    if n_past:
        for copy in past_copies(0, 0):
            copy.start()

        def past_body(state):
            n = state[0]
            slot = n & 1
            for copy in past_copies(n, slot):
                copy.wait()

            @pl.when(n + 1 < n_past)
            def _():
                for copy in past_copies(n + 1, 1 - slot):
                    copy.start()

            return n + 1, step(0, [past_block(slot)])

        n_used, _ = lax.while_loop(unfinished(n_past), past_body, (jnp.int32(0), smallest))

        @pl.when(n_used < n_past)
        def _():
            for copy in past_copies(n_used, n_used & 1):
                copy.wait()

    for c in range(chains):
        for p in range(HEAD_PAIRS):
            o_ref[0, c * sub:(c + 1) * sub, p * LANES:(p + 1) * LANES] = acc_ref[c, p].astype(BF16)


def _attention(q, kbt_new, vbd_new, um, *, tq, kt_cache=None, vt_cache=None, layer=0):
    nb, t, _ = q.shape
    tn2 = vbd_new.shape[1]
    n_past = 0 if kt_cache is None else kt_cache.shape[3] // KEY_BLOCK
    chains = max(1, tq // KEY_BLOCK)
    sub = tq // chains
    assert chains in (1, 2) and tq == chains * sub and (chains == 1 or sub == KEY_BLOCK)
    grid = (nb, t // tq)
    q_spec = pl.BlockSpec((1, tq, ATTN_W), lambda b, i: (b, i, 0))
    in_specs = [q_spec, pl.BlockSpec((1, ATTN_W, tn2), lambda b, i: (b, 0, 0)),
                pl.BlockSpec((1, tn2, ATTN_W), lambda b, i: (b, 0, 0)), _const_spec(um.shape)]
    args = [q, kbt_new, vbd_new, um]
    scratch = [pltpu.VMEM((chains, HEAD_PAIRS, sub, 2 * KEY_BLOCK), F32),
               pltpu.VMEM((chains, HEAD_PAIRS, sub, LANES), F32),
               pltpu.VMEM((chains, 2 * HEAD_PAIRS * sub, 2 * KEY_BLOCK), BF16),
               pltpu.VMEM((chains, 2 * HEAD_PAIRS * sub, 2 * KEY_BLOCK), F32)]
    if n_past:
        assert chains == 1 and kt_cache.shape[3] % KEY_BLOCK == 0 and kt_cache.shape[1:3] == (nb, ATTN_W)
        in_specs += [pl.BlockSpec(memory_space=pl.ANY)] * 2
        args += [kt_cache, vt_cache]
        scratch += [pltpu.VMEM((2, ATTN_W, KEY_BLOCK), F32), pltpu.VMEM((2, ATTN_W, KEY_BLOCK), F32),
                    pltpu.SemaphoreType.DMA((2, 2))]
    return pl.pallas_call(
        functools.partial(_attn_kernel, tq=tq, n_past=n_past, layer=layer),
        grid=grid,
        in_specs=in_specs,
        out_specs=q_spec,
        out_shape=jax.ShapeDtypeStruct((nb, t, ATTN_W), BF16),
        scratch_shapes=scratch,
        compiler_params=_compiler_params(),
        name="attention",
    )(*args)


def _mix_kernel(x_ref, a_ref, u_ref, uh_ref, st_ref, sga_ref, sgc_ref, dww_ref, dwb_ref, cng_ref,
                cnb_ref, wco_ref, bco_ref, wao_ref, wo_ref, x1_ref, ext_ref, act_ref):
    i = pl.program_id(1)
    bb, tm, _ = x_ref.shape
    rows = bb * tm

    @pl.when(i == 0)
    def _():
        ext_ref[0, :, 0:CONV_HALO, :] = st_ref[...]

    @pl.when(i > 0)
    def _():
        ext_ref[0, :, 0:CONV_HALO, :] = uh_ref[...]

    ext_ref[0, :, CONV_HALO:CONV_HALO + tm, :] = u_ref[...]
    span = CONV_HALO + tm - SUBLANES
    for s in range(1, SUBLANES):
        ext_ref[s, :, 0:span, :] = ext_ref[0, :, s:s + span, :]

    chunk = 32
    first = CONV_HALO - (CONV_W - 1)

    def conv_rows(b, r0):
        acc = jnp.broadcast_to(dwb_ref[...], (chunk // SUBLANES, SUBLANES, CONV_CH))
        for j in range(CONV_W):
            s, base = (first + j) % SUBLANES, (first + j) // SUBLANES * SUBLANES
            taps = ext_ref[s, b, r0 + base:r0 + base + chunk, :]
            acc = acc + taps.reshape(chunk // SUBLANES, SUBLANES, CONV_CH) * dww_ref[j]
        acc = acc.reshape(chunk, CONV_CH)
        mu = jnp.mean(acc, axis=-1, keepdims=True)
        xc = acc - mu
        y = xc * lax.rsqrt(jnp.mean(xc * xc, axis=-1, keepdims=True) + EPS)
        y = y * cng_ref[...] + cnb_ref[...]
        act_ref[b * tm + r0:b * tm + r0 + chunk, :] = (y * _sigmoid(y)).astype(BF16)

    a_heads = a_ref[...].reshape(rows, ATTN_W)
    col_tiles = [slice(n, n + MXU_WIDTH) for n in range(0, D_MODEL, MXU_WIDTH)]
    a_cols = []
    for b in range(bb):
        for r0 in range(0, tm, chunk):
            conv_rows(b, r0)
            if len(a_cols) < len(col_tiles):
                a_cols.append(jnp.dot(a_heads, wao_ref[:, col_tiles[len(a_cols)]], preferred_element_type=F32))
    assert len(a_cols) == len(col_tiles)
    a = jnp.concatenate(a_cols, axis=1)
    c = jnp.dot(act_ref[...], wco_ref[...], preferred_element_type=F32) + bco_ref[...]
    m = (sga_ref[...].reshape(rows, D_MODEL).astype(F32) * a
         + sgc_ref[...].reshape(rows, D_MODEL).astype(F32) * c)
    x1 = x_ref[...].reshape(rows, D_MODEL) + jnp.dot(m.astype(BF16), wo_ref[...],
                                                     preferred_element_type=F32)
    x1_ref[...] = x1.reshape(bb, tm, D_MODEL)


def _mix(x, a, u, state, sga, sgc, dww, dwb, cng, cnb, wco, bco, wao, wo, *, layer, bb, tm):
    nb, t, _ = x.shape
    grid = (nb // bb, t // tm)
    row = lambda width: pl.BlockSpec((bb, tm, width), lambda b, i: (b, i, 0))
    halo_blocks = tm // CONV_HALO
    halo = pl.BlockSpec((bb, CONV_HALO, CONV_CH),
                        lambda b, i: (b, jnp.maximum(i * halo_blocks - 1, 0), 0))
    st_spec = pl.BlockSpec((bb, CONV_HALO, CONV_CH), lambda b, i: (b, 0, 0))
    return pl.pallas_call(
        _mix_kernel,
        grid=grid,
        in_specs=[row(D_MODEL), row(ATTN_W), row(CONV_CH), halo, st_spec, row(D_MODEL), row(D_MODEL),
                  _layer_spec((CONV_W, SUBLANES, CONV_CH), layer), _layer_spec((1, CONV_CH), layer),
                  _layer_spec((1, CONV_CH), layer), _layer_spec((1, CONV_CH), layer),
                  _layer_spec((CONV_CH, D_MODEL), layer), _layer_spec((1, D_MODEL), layer),
                  _layer_spec((ATTN_W, D_MODEL), layer), _layer_spec((D_MODEL, D_MODEL), layer)],
        out_specs=row(D_MODEL),
        out_shape=jax.ShapeDtypeStruct((nb, t, D_MODEL), F32),
        scratch_shapes=[pltpu.VMEM((SUBLANES, bb, CONV_HALO + tm, CONV_CH), F32),
                        pltpu.VMEM((bb * tm, CONV_CH), BF16)],
        compiler_params=_compiler_params(),
        name="mix",
    )(x, a, u, u, state, sga, sgc, dww, dwb, cng, cnb, wco, bco, wao, wo)


def _ffn_kernel(x1_ref, st_ref, g2_ref, wup_ref, fdw_ref, fdb_ref, wdn_ref, fg_ref,
                y_ref, nf_ref, prev_ref, act_ref, *, final_norm):
    i = pl.program_id(1)
    bb, tm, _ = x1_ref.shape
    rows = bb * tm

    @pl.when(i == 0)
    def _():
        prev_ref[...] = st_ref[...]

    x1 = x1_ref[...].reshape(rows, D_MODEL)
    h = x1 * lax.rsqrt(jnp.mean(x1 * x1, axis=-1, keepdims=True) + EPS) * g2_ref[...]
    h = h.astype(BF16)

    first_rows = lax.broadcasted_iota(jnp.int32, (bb, FFN_HALO, FFN_CHUNK), 1)

    def conv(c0):
        up = jnp.dot(h, wup_ref[:, c0:c0 + FFN_CHUNK], preferred_element_type=F32)
        up = up.reshape(bb, tm, FFN_CHUNK)
        before = prev_ref[:, :, c0:c0 + FFN_CHUNK]
        tail = up[:, tm - FFN_HALO:tm, :]
        prev_ref[:, :, c0:c0 + FFN_CHUNK] = tail
        nf_ref[:, :, c0:c0 + FFN_CHUNK] = tail
        w = fdw_ref[:, c0:c0 + FFN_CHUNK]
        out = up * w[FFN_CONV_W - 1:FFN_CONV_W] + fdb_ref[:, c0:c0 + FFN_CHUNK]
        for back in range(1, FFN_CONV_W):
            moved = pltpu.roll(up, back, axis=1)
            head = jnp.where(first_rows < back, pltpu.roll(before, back, axis=1), moved[:, 0:FFN_HALO, :])
            moved = jnp.concatenate([head, moved[:, FFN_HALO:, :]], axis=1)
            out = out + moved * w[FFN_CONV_W - 1 - back:FFN_CONV_W - back]
        return out.reshape(rows, FFN_CHUNK)

    for c in range(D_FF // FFN_CHUNK):
        gate = conv(c * FFN_CHUNK)
        val = conv(D_FF + c * FFN_CHUNK)
        act_ref[:, c * FFN_CHUNK:(c + 1) * FFN_CHUNK] = (gate * _sigmoid(gate) * val).astype(BF16)

    y = x1 + jnp.dot(act_ref[...], wdn_ref[...], preferred_element_type=F32)
    if final_norm:
        y = y * lax.rsqrt(jnp.mean(y * y, axis=-1, keepdims=True) + EPS) * fg_ref[...]
    y_ref[...] = y.reshape(bb, tm, D_MODEL)


def _ffn(x1, state, g2, wup, fdw, fdb, wdn, fg, *, layer, bb, tm, final_norm):
    nb, t, _ = x1.shape
    grid = (nb // bb, t // tm)
    row = pl.BlockSpec((bb, tm, D_MODEL), lambda b, i: (b, i, 0))
    st_spec = pl.BlockSpec((bb, FFN_HALO, 2 * D_FF), lambda b, i: (b, 0, 0))
    return pl.pallas_call(
        functools.partial(_ffn_kernel, final_norm=final_norm),
        grid=grid,
        in_specs=[row, st_spec, _layer_spec((1, D_MODEL), layer), _layer_spec((D_MODEL, 2 * D_FF), layer),
                  _layer_spec((FFN_CONV_W, 2 * D_FF), layer), _layer_spec((1, 2 * D_FF), layer),
                  _layer_spec((D_FF, D_MODEL), layer), _const_spec((1, D_MODEL))],
        out_specs=[row, st_spec],
        out_shape=[jax.ShapeDtypeStruct((nb, t, D_MODEL), F32),
                   jax.ShapeDtypeStruct((nb, FFN_HALO, 2 * D_FF), F32)],
        scratch_shapes=[pltpu.VMEM((bb, FFN_HALO, 2 * D_FF), F32),
                        pltpu.VMEM((bb * tm, D_FF), BF16)],
        compiler_params=_compiler_params(),
        name="ffn",
    )(x1, state, g2, wup, fdw, fdb, wdn, fg)


def _cumsum_matrix():
    j = np.arange(2 * KEY_BLOCK)[:, None]
    s = np.arange(2 * KEY_BLOCK)[None, :]
    same = (j // KEY_BLOCK) == (s // KEY_BLOCK)
    return (same & ((j % KEY_BLOCK) > (s % KEY_BLOCK))).astype(np.float32)


def _head_mean_matrix():
    h = np.arange(ATTN_W) // HEAD_DIM
    return (h[:, None] == h[None, :]).astype(np.float32)


def _pad_rows_front(a, rows):
    return jnp.pad(a, ((0, 0), (rows - a.shape[1], 0), (0, 0)))


def kernel(x_prompt, x_sample, cache_sb_k, cache_sb_v, state_conv, state_ffn_conv, norm1_g, w_in, b_in,
           q_norm_g, k_norm_g, w_attn_out, dw_w, dw_b, cn_g, cn_b, w_conv_out, b_conv_out, w_out,
           norm2_g, w_up, ffn_dw_w, ffn_dw_b, w_down, final_g):
    depth = w_in.shape[0]
    bp, seq, _ = x_prompt.shape
    bs, dec, _ = x_sample.shape
    past = cache_sb_k.shape[2]
    um = jnp.asarray(_cumsum_matrix(), BF16)
    hm = jnp.asarray(_head_mean_matrix(), BF16)

    feature_major = lambda c: jnp.transpose(c, (0, 1, 3, 4, 2)).reshape(depth, bs, ATTN_W, past)
    kt_cache, vt_cache = feature_major(cache_sb_k), feature_major(cache_sb_v)
    conv_zero = jnp.zeros((bp, CONV_HALO, CONV_CH), F32)
    ffn_zero = jnp.zeros((bp, FFN_HALO, 2 * D_FF), F32)

    vec = lambda a: a.reshape(depth, 1, -1)
    per_head = lambda g: vec(jnp.tile(g, (1, N_HEADS)))
    proj_params = (vec(norm1_g), w_in.astype(BF16), vec(b_in), per_head(q_norm_g), per_head(k_norm_g))
    mix_params = (jnp.broadcast_to(dw_w[:, :, None, :], (depth, CONV_W, SUBLANES, CONV_CH)), vec(dw_b), vec(cn_g),
                  vec(cn_b), w_conv_out.astype(BF16), vec(b_conv_out), w_attn_out.astype(BF16), w_out.astype(BF16))
    ffn_params = (vec(norm2_g), w_up.astype(BF16), ffn_dw_w, vec(ffn_dw_b), w_down.astype(BF16))

    def run(l, x, conv_state, ffn_state, tiles, kv_stack=None):
        bb = tiles.batch
        q, k, v, kbt, vbd, u, sga, sgc = _inproj(x, *proj_params, hm, layer=l, bb=bb, tm=tiles.proj,
                                                 kv_stack=kv_stack)
        t = x.shape[1]
        if kv_stack is None:
            a = _attention(q, kbt, vbd, um, tq=tiles.attn, kt_cache=kt_cache, vt_cache=vt_cache, layer=l)
        else:
            a = _attention(q, kbt, vbd, um, tq=tiles.attn)
        x1 = _mix(x, a, u, conv_state, sga, sgc, *mix_params, layer=l, bb=bb, tm=tiles.mix)
        y, nf = _ffn(x1, ffn_state, *ffn_params, final_g.reshape(1, -1), layer=l, bb=bb, tm=tiles.ffn,
                     final_norm=l == depth - 1)
        return y, k, v, u[:, t - (CONV_W - 1):, :], nf[:, FFN_HALO - (FFN_CONV_W - 1):, :]

    xp, xs = x_prompt, x_sample
    kp, vp = jnp.zeros((depth, bp, ATTN_W, seq), F32), jnp.zeros((depth, bp, ATTN_W, seq), F32)
    outs = {name: [] for name in ("ks", "vs", "cp", "cs", "fp", "fs")}
    for l in range(depth):
        xp, kp, vp, c1, f1 = run(l, xp, conv_zero, ffn_zero, PROMPT_TILES, kv_stack=(kp, vp))
        xs, k2, v2, c2, f2 = run(l, xs, _pad_rows_front(state_conv[l], CONV_HALO),
                                 _pad_rows_front(state_ffn_conv[l], FFN_HALO),
                                 Tiles(batch=bs, proj=dec, attn=dec, mix=dec, ffn=dec))
        for name, val in zip(("ks", "vs", "cp", "cs", "fp", "fs"), (k2, v2, c1, c2, f1, f2)):
            outs[name].append(val)

    heads = lambda lst, b, t: jnp.stack(lst).reshape(depth, b, t, N_HEADS, HEAD_DIM)
    heads_t = lambda a: jnp.transpose(a.reshape(depth, bp, N_HEADS, HEAD_DIM, seq), (0, 1, 4, 2, 3))
    return (xp, xs, heads_t(kp), heads_t(vp), heads(outs["ks"], bs, dec), heads(outs["vs"], bs, dec),
            jnp.stack(outs["cp"]), jnp.stack(outs["cs"]), jnp.stack(outs["fp"]), jnp.stack(outs["fs"]))
```

```python
import functools
from typing import NamedTuple

import numpy as np
import jax
import jax.numpy as jnp
from jax import lax
from jax.experimental import pallas as pl
from jax.experimental.pallas import tpu as pltpu

D_MODEL = 1024
N_HEADS = 8
HEAD_DIM = 64
ATTN_W = N_HEADS * HEAD_DIM
CONV_CH = 512
CONV_W = 31
D_FF = 2816
FFN_CONV_W = 3
N_IN = 3 * ATTN_W + 2 * CONV_CH + 2 * D_MODEL
EPS = 1e-6

LANES = 128
SUBLANES = 8
MXU_WIDTH = 256
KEY_BLOCK = 128
HEAD_PAIRS = ATTN_W // LANES
CONV_HALO = 32
CONV_ROWS = 32
FFN_HALO = SUBLANES
FFN_CHUNK = 256


class Tiles(NamedTuple):
    batch: int
    proj: int
    attn: int
    mix: int
    ffn: int


PROMPT_TILES = Tiles(batch=1, proj=512, attn=512, mix=256, ffn=256)
VMEM_LIMIT = 56 * 1024 * 1024
LOG2E = 1.4426950408889634
CARRY_LIMIT = 152.0

F32 = jnp.float32
BF16 = jnp.bfloat16


def _const_spec(shape):
    return pl.BlockSpec(shape, lambda *_: (0,) * len(shape), pipeline_mode=pl.Buffered(1))


def _compiler_params():
    return pltpu.CompilerParams(dimension_semantics=("arbitrary", "arbitrary"), vmem_limit_bytes=VMEM_LIMIT)


def _layer_spec(shape, layer):
    return pl.BlockSpec((None,) + tuple(shape), lambda *_: (layer,) + (0,) * len(shape),
                        pipeline_mode=pl.Buffered(1))


def _sigmoid(x):
    return 1.0 / (1.0 + jnp.exp(-x))


def _inproj_kernel(x_ref, g1_ref, win_ref, bin_ref, qg_ref, kg_ref, hm_ref, *refs, feature_major):
    q_ref, k_ref, v_ref, kbt_ref, vbd_ref, u_ref, sga_ref, sgc_ref = refs[-8:]
    bb, tm, _ = x_ref.shape
    rows = bb * tm
    x = x_ref[...].reshape(rows, D_MODEL)
    h = x * lax.rsqrt(jnp.mean(x * x, axis=-1, keepdims=True) + EPS) * g1_ref[...]
    h = h.astype(BF16)

    def proj(c0, width):
        return (jnp.dot(h, win_ref[:, c0:c0 + width], preferred_element_type=F32)
                + bin_ref[:, c0:c0 + width])

    def head_rmsnorm(y, g):
        ssum = jnp.dot((y * y).astype(BF16), hm_ref[...], preferred_element_type=F32)
        return y * lax.rsqrt(ssum * (1.0 / HEAD_DIM) + EPS) * g

    q = head_rmsnorm(proj(0, ATTN_W), qg_ref[...])
    q_ref[...] = (q * (HEAD_DIM ** -0.5 * LOG2E)).astype(BF16).reshape(bb, tm, ATTN_W)
    k = head_rmsnorm(proj(ATTN_W, ATTN_W), kg_ref[...])
    v = proj(2 * ATTN_W, ATTN_W)
    if feature_major:
        for b in range(bb):
            k_ref[b] = k[b * tm:(b + 1) * tm].T
            v_ref[b] = v[b * tm:(b + 1) * tm].T
    else:
        k_ref[...] = k.reshape(bb, tm, ATTN_W)
        v_ref[...] = v.reshape(bb, tm, ATTN_W)

    blk = min(tm, KEY_BLOCK)
    even_row = (lax.broadcasted_iota(jnp.int32, (ATTN_W, KEY_BLOCK), 0) & (LANES - 1)) < HEAD_DIM
    even_lane = (lax.broadcasted_iota(jnp.int32, (KEY_BLOCK, ATTN_W), 1) & (LANES - 1)) < HEAD_DIM
    for b in range(bb):
        for c in range(tm // blk):
            r0 = b * tm + c * blk
            kb, vb = k[r0:r0 + blk], v[r0:r0 + blk]
            if blk < KEY_BLOCK:
                fill = jnp.zeros((KEY_BLOCK - blk, ATTN_W), F32)
                kb, vb = jnp.concatenate([kb, fill], axis=0), jnp.concatenate([vb, fill], axis=0)
            kt = kb.T
            o = 2 * c * KEY_BLOCK
            kbt_ref[b, :, o:o + KEY_BLOCK] = jnp.where(even_row, kt, 0.0).astype(BF16)
            kbt_ref[b, :, o + KEY_BLOCK:o + 2 * KEY_BLOCK] = jnp.where(even_row, 0.0, kt).astype(BF16)
            vbd_ref[b, o:o + KEY_BLOCK, :] = jnp.where(even_lane, vb, 0.0).astype(BF16)
            vbd_ref[b, o + KEY_BLOCK:o + 2 * KEY_BLOCK, :] = jnp.where(even_lane, 0.0, vb).astype(BF16)

    c0 = 3 * ATTN_W
    u = proj(c0, CONV_CH) * _sigmoid(proj(c0 + CONV_CH, CONV_CH))
    u_ref[...] = u.reshape(bb, tm, CONV_CH)
    c0 += 2 * CONV_CH
    sga_ref[...] = _sigmoid(proj(c0, D_MODEL)).astype(BF16).reshape(bb, tm, D_MODEL)
    sgc_ref[...] = _sigmoid(proj(c0 + D_MODEL, D_MODEL)).astype(BF16).reshape(bb, tm, D_MODEL)


def _inproj(x, g1, win, b_in, qg, kg, hm, *, layer, bb, tm, kv_stack=None):
    nb, t, _ = x.shape
    grid = (nb // bb, t // tm)
    row = lambda width: pl.BlockSpec((bb, tm, width), lambda b, i: (b, i, 0))
    shp = lambda width, dt: jax.ShapeDtypeStruct((nb, t, width), dt)
    kw = 2 * KEY_BLOCK * (tm // min(tm, KEY_BLOCK))
    in_specs = [row(D_MODEL), _layer_spec((1, D_MODEL), layer), _layer_spec((D_MODEL, N_IN), layer),
                _layer_spec((1, N_IN), layer), _layer_spec((1, ATTN_W), layer),
                _layer_spec((1, ATTN_W), layer), _const_spec((ATTN_W, ATTN_W))]
    args = [x, g1, win, b_in, qg, kg, hm]
    aliases = {}
    if kv_stack is not None:
        kv_spec = pl.BlockSpec((None, bb, ATTN_W, tm), lambda b, i: (layer, b, 0, i))
        kv_shape = jax.ShapeDtypeStruct(kv_stack[0].shape, F32)
        aliases = {len(args): 1, len(args) + 1: 2}
        in_specs += [pl.BlockSpec(memory_space=pl.ANY)] * 2
        args += list(kv_stack)
    else:
        kv_spec, kv_shape = row(ATTN_W), shp(ATTN_W, F32)
    return pl.pallas_call(
        functools.partial(_inproj_kernel, feature_major=kv_stack is not None),
        grid=grid,
        in_specs=in_specs,
        input_output_aliases=aliases,
        out_specs=[row(ATTN_W), kv_spec, kv_spec,
                   pl.BlockSpec((bb, ATTN_W, kw), lambda b, i: (b, 0, i)),
                   pl.BlockSpec((bb, kw, ATTN_W), lambda b, i: (b, i, 0)),
                   row(CONV_CH), row(D_MODEL), row(D_MODEL)],
        out_shape=[shp(ATTN_W, BF16), kv_shape, kv_shape,
                   jax.ShapeDtypeStruct((nb, ATTN_W, kw * (t // tm)), BF16),
                   jax.ShapeDtypeStruct((nb, kw * (t // tm), ATTN_W), BF16),
                   shp(CONV_CH, F32), shp(D_MODEL, BF16), shp(D_MODEL, BF16)],
        compiler_params=_compiler_params(),
        name="inproj",
    )(*args)


def _attn_kernel(*refs, tq, n_past, layer):
    if n_past:
        (q_ref, kn_ref, vn_ref, um_ref, kp_ref, vp_ref, o_ref,
         carry_ref, acc_ref, sp_ref, zs_ref, kbuf_ref, vbuf_ref, sem_ref) = refs
    else:
        q_ref, kn_ref, vn_ref, um_ref, o_ref, carry_ref, acc_ref, sp_ref, zs_ref = refs
    batch = pl.program_id(0)
    i = pl.program_id(1)
    chains, sub = carry_ref.shape[0], carry_ref.shape[2]
    carry_ref[...] = jnp.zeros_like(carry_ref)
    acc_ref[...] = jnp.zeros_like(acc_ref)

    def step(c, blocks, key0=None):
        rows = slice(c * sub, (c + 1) * sub)
        if key0 is None:
            mask = None
        else:
            qpos = i * tq + c * sub + lax.broadcasted_iota(jnp.int32, (sub, 2 * KEY_BLOCK), 0)
            col = lax.broadcasted_iota(jnp.int32, (sub, 2 * KEY_BLOCK), 1) & (KEY_BLOCK - 1)
            mask = key0 + col < qpos
        part = lambda j, p: slice((j * HEAD_PAIRS + p) * sub, (j * HEAD_PAIRS + p + 1) * sub)
        for j, (scores, _) in enumerate(blocks):
            for p in range(HEAD_PAIRS):
                z = scores(p, rows)
                sp = jnp.where(z > 64.0, z, jnp.log(1.0 + jnp.exp2(z)) * LOG2E)
                if mask is not None:
                    sp = jnp.where(mask, sp, 0.0)
                sp_ref[c, part(j, p), :] = sp.astype(BF16)
                zs_ref[c, part(j, p), :] = z - sp
        later_all = jnp.dot(sp_ref[c, 0:len(blocks) * HEAD_PAIRS * sub, :], um_ref[...],
                            preferred_element_type=F32)
        for j, (_, weighted) in enumerate(blocks):
            smallest = None
            for p in range(HEAD_PAIRS):
                later = later_all[part(j, p)]
                carry = carry_ref[c, p]
                w = jnp.exp2(zs_ref[c, part(j, p), :] - later - carry)
                if mask is not None:
                    w = jnp.where(mask, w, 0.0)
                acc_ref[c, p] += weighted(p, w.astype(BF16))
                halves = []
                for c0 in (0, KEY_BLOCK):
                    own = sp_ref[c, part(j, p), c0:c0 + KEY_BLOCK].astype(F32)
                    halves.append(jnp.broadcast_to(later[:, c0:c0 + 1] + own[:, 0:1], (sub, KEY_BLOCK)))
                carry = carry + jnp.concatenate(halves, axis=1)
                carry_ref[c, p] = carry
                smallest = carry if smallest is None else jnp.minimum(smallest, carry)
        return jnp.min(smallest)

    def new_block(jb):
        start = pl.multiple_of(jb * 2 * KEY_BLOCK, 2 * KEY_BLOCK)
        scores = lambda p, rows: jnp.dot(q_ref[0, rows, p * LANES:(p + 1) * LANES],
                                         kn_ref[0, p * LANES:(p + 1) * LANES, pl.ds(start, 2 * KEY_BLOCK)],
                                         preferred_element_type=F32)
        weighted = lambda p, w: jnp.dot(w, vn_ref[0, pl.ds(start, 2 * KEY_BLOCK), p * LANES:(p + 1) * LANES],
                                        preferred_element_type=F32)
        return scores, weighted

    def past_block(slot):
        even_feature = lax.broadcasted_iota(jnp.int32, (LANES, KEY_BLOCK), 0) < HEAD_DIM

        def split(buf_ref, p):
            blk = buf_ref[slot, p * LANES:(p + 1) * LANES, :]
            return jnp.concatenate([jnp.where(even_feature, blk, 0.0).astype(BF16),
                                    jnp.where(even_feature, 0.0, blk).astype(BF16)], axis=1)

        scores = lambda p, rows: jnp.dot(q_ref[0, rows, p * LANES:(p + 1) * LANES], split(kbuf_ref, p),
                                         preferred_element_type=F32)
        weighted = lambda p, w: lax.dot_general(w, split(vbuf_ref, p), (((1,), (1,)), ((), ())),
                                                preferred_element_type=F32)
        return scores, weighted

    def past_copies(n, slot):
        start = pl.multiple_of((n_past - 1 - n) * KEY_BLOCK, KEY_BLOCK)
        return [pltpu.make_async_copy(src.at[layer, batch, :, pl.ds(start, KEY_BLOCK)], dst.at[slot],
                                      sem_ref.at[which, slot])
                for which, (src, dst) in enumerate(((kp_ref, kbuf_ref), (vp_ref, vbuf_ref)))]

    n_open = [(i * tq + c * sub) // KEY_BLOCK for c in range(chains)]

    def unfinished(n_blocks):
        return lambda state: jnp.logical_and(state[0] < n_blocks, state[1] < CARRY_LIMIT)

    smallest = [step(c, [new_block(n_open[c])], key0=n_open[c] * KEY_BLOCK) for c in range(chains)]
    if chains > 1:
        older = lambda c, k: [new_block(n_open[c] - 1 - k), new_block(n_open[c] - 2 - k)]
        k_all, *smallest = lax.while_loop(
            lambda state: jnp.logical_and(state[0] + 2 <= n_open[0],
                                          functools.reduce(jnp.minimum, state[1:]) < CARRY_LIMIT),
            lambda state: (state[0] + 2, *[step(c, older(c, state[0])) for c in range(chains)]),
            (jnp.int32(0), *smallest))
        for c in range(1, chains):
            lax.while_loop(unfinished(n_open[c]),
                           lambda state, c=c: (state[0] + 1, step(c, [new_block(n_open[c] - 1 - state[0])])),
                           (k_all, smallest[c]))
    else:
        _, smallest = lax.while_loop(
            unfinished(n_open[0]),
            lambda state: (state[0] + 1, step(0, [new_block(n_open[0] - 1 - state[0])])),
            (jnp.int32(0), smallest[0]))

    if n_past:
        for copy in past_copies(0, 0):
            copy.start()

        def past_body(state):
            n = state[0]
            slot = n & 1
            for copy in past_copies(n, slot):
                copy.wait()

            @pl.when(n + 1 < n_past)
            def _():
                for copy in past_copies(n + 1, 1 - slot):
                    copy.start()

            return n + 1, step(0, [past_block(slot)])

        n_used, _ = lax.while_loop(unfinished(n_past), past_body, (jnp.int32(0), smallest))

        @pl.when(n_used < n_past)
        def _():
            for copy in past_copies(n_used, n_used & 1):
                copy.wait()

    for c in range(chains):
        for p in range(HEAD_PAIRS):
            o_ref[0, c * sub:(c + 1) * sub, p * LANES:(p + 1) * LANES] = acc_ref[c, p].astype(BF16)


def _attention(q, kbt_new, vbd_new, um, *, tq, kt_cache=None, vt_cache=None, layer=0):
    nb, t, _ = q.shape
    tn2 = vbd_new.shape[1]
    n_past = 0 if kt_cache is None else kt_cache.shape[3] // KEY_BLOCK
    chains = max(1, tq // KEY_BLOCK)
    sub = tq // chains
    assert tq == chains * sub and (chains == 1 or (sub == KEY_BLOCK and chains % 2 == 0))
    grid = (nb, t // tq)
    q_spec = pl.BlockSpec((1, tq, ATTN_W), lambda b, i: (b, i, 0))
    in_specs = [q_spec, pl.BlockSpec((1, ATTN_W, tn2), lambda b, i: (b, 0, 0)),
                pl.BlockSpec((1, tn2, ATTN_W), lambda b, i: (b, 0, 0)), _const_spec(um.shape)]
    args = [q, kbt_new, vbd_new, um]
    scratch = [pltpu.VMEM((chains, HEAD_PAIRS, sub, 2 * KEY_BLOCK), F32),
               pltpu.VMEM((chains, HEAD_PAIRS, sub, LANES), F32),
               pltpu.VMEM((chains, 2 * HEAD_PAIRS * sub, 2 * KEY_BLOCK), BF16),
               pltpu.VMEM((chains, 2 * HEAD_PAIRS * sub, 2 * KEY_BLOCK), F32)]
    if n_past:
        assert chains == 1 and kt_cache.shape[3] % KEY_BLOCK == 0 and kt_cache.shape[1:3] == (nb, ATTN_W)
        in_specs += [pl.BlockSpec(memory_space=pl.ANY)] * 2
        args += [kt_cache, vt_cache]
        scratch += [pltpu.VMEM((2, ATTN_W, KEY_BLOCK), F32), pltpu.VMEM((2, ATTN_W, KEY_BLOCK), F32),
                    pltpu.SemaphoreType.DMA((2, 2))]
    return pl.pallas_call(
        functools.partial(_attn_kernel, tq=tq, n_past=n_past, layer=layer),
        grid=grid,
        in_specs=in_specs,
        out_specs=q_spec,
        out_shape=jax.ShapeDtypeStruct((nb, t, ATTN_W), BF16),
        scratch_shapes=scratch,
        compiler_params=_compiler_params(),
        name="attention",
    )(*args)


def _mix_kernel(x_ref, a_ref, u_ref, uh_ref, st_ref, sga_ref, sgc_ref, dww_ref, dwb_ref, cng_ref,
                cnb_ref, wco_ref, bco_ref, wao_ref, wo_ref, x1_ref, ext_ref, act_ref):
    i = pl.program_id(1)
    bb, tm, _ = x_ref.shape
    rows = bb * tm

    @pl.when(i == 0)
    def _():
        ext_ref[0, :, 0:CONV_HALO, :] = st_ref[...]

    @pl.when(i > 0)
    def _():
        ext_ref[0, :, 0:CONV_HALO, :] = uh_ref[...]

    ext_ref[0, :, CONV_HALO:CONV_HALO + tm, :] = u_ref[...]
    span = CONV_HALO + tm - SUBLANES
    for s in range(1, SUBLANES):
        ext_ref[s, :, 0:span, :] = ext_ref[0, :, s:s + span, :]

    chunk = CONV_ROWS
    first = CONV_HALO - (CONV_W - 1)

    def conv_rows(b, r0):
        acc = jnp.broadcast_to(dwb_ref[...], (chunk // SUBLANES, SUBLANES, CONV_CH))
        for j in range(CONV_W):
            s, base = (first + j) % SUBLANES, (first + j) // SUBLANES * SUBLANES
            taps = ext_ref[s, b, r0 + base:r0 + base + chunk, :]
            acc = acc + taps.reshape(chunk // SUBLANES, SUBLANES, CONV_CH) * dww_ref[j]
        acc = acc.reshape(chunk, CONV_CH)
        mu = jnp.mean(acc, axis=-1, keepdims=True)
        xc = acc - mu
        y = xc * lax.rsqrt(jnp.mean(xc * xc, axis=-1, keepdims=True) + EPS)
        y = y * cng_ref[...] + cnb_ref[...]
        act_ref[b * tm + r0:b * tm + r0 + chunk, :] = (y * _sigmoid(y)).astype(BF16)

    a_heads = a_ref[...].reshape(rows, ATTN_W)
    col_tiles = [slice(n, n + MXU_WIDTH) for n in range(0, D_MODEL, MXU_WIDTH)]
    a_cols = []
    for b in range(bb):
        for r0 in range(0, tm, chunk):
            conv_rows(b, r0)
            if len(a_cols) < len(col_tiles):
                a_cols.append(jnp.dot(a_heads, wao_ref[:, col_tiles[len(a_cols)]], preferred_element_type=F32))
    assert len(a_cols) == len(col_tiles)
    a = jnp.concatenate(a_cols, axis=1)
    c = jnp.dot(act_ref[...], wco_ref[...], preferred_element_type=F32) + bco_ref[...]
    m = (sga_ref[...].reshape(rows, D_MODEL).astype(F32) * a
         + sgc_ref[...].reshape(rows, D_MODEL).astype(F32) * c)
    x1 = x_ref[...].reshape(rows, D_MODEL) + jnp.dot(m.astype(BF16), wo_ref[...],
                                                     preferred_element_type=F32)
    x1_ref[...] = x1.reshape(bb, tm, D_MODEL)


def _mix(x, a, u, state, sga, sgc, dww, dwb, cng, cnb, wco, bco, wao, wo, *, layer, bb, tm):
    nb, t, _ = x.shape
    grid = (nb // bb, t // tm)
    row = lambda width: pl.BlockSpec((bb, tm, width), lambda b, i: (b, i, 0))
    halo_blocks = tm // CONV_HALO
    halo = pl.BlockSpec((bb, CONV_HALO, CONV_CH),
                        lambda b, i: (b, jnp.maximum(i * halo_blocks - 1, 0), 0))
    st_spec = pl.BlockSpec((bb, CONV_HALO, CONV_CH), lambda b, i: (b, 0, 0))
    return pl.pallas_call(
        _mix_kernel,
        grid=grid,
        in_specs=[row(D_MODEL), row(ATTN_W), row(CONV_CH), halo, st_spec, row(D_MODEL), row(D_MODEL),
                  _layer_spec((CONV_W, SUBLANES, CONV_CH), layer), _layer_spec((1, CONV_CH), layer),
                  _layer_spec((1, CONV_CH), layer), _layer_spec((1, CONV_CH), layer),
                  _layer_spec((CONV_CH, D_MODEL), layer), _layer_spec((1, D_MODEL), layer),
                  _layer_spec((ATTN_W, D_MODEL), layer), _layer_spec((D_MODEL, D_MODEL), layer)],
        out_specs=row(D_MODEL),
        out_shape=jax.ShapeDtypeStruct((nb, t, D_MODEL), F32),
        scratch_shapes=[pltpu.VMEM((SUBLANES, bb, CONV_HALO + tm, CONV_CH), F32),
                        pltpu.VMEM((bb * tm, CONV_CH), BF16)],
        compiler_params=_compiler_params(),
        name="mix",
    )(x, a, u, u, state, sga, sgc, dww, dwb, cng, cnb, wco, bco, wao, wo)


def _ffn_kernel(x1_ref, st_ref, g2_ref, wup_ref, fdw_ref, fdb_ref, wdn_ref, fg_ref,
                y_ref, nf_ref, prev_ref, act_ref, *, final_norm):
    i = pl.program_id(1)
    bb, tm, _ = x1_ref.shape
    rows = bb * tm

    @pl.when(i == 0)
    def _():
        prev_ref[...] = st_ref[...]

    x1 = x1_ref[...].reshape(rows, D_MODEL)
    h = x1 * lax.rsqrt(jnp.mean(x1 * x1, axis=-1, keepdims=True) + EPS) * g2_ref[...]
    h = h.astype(BF16)

    first_rows = lax.broadcasted_iota(jnp.int32, (bb, FFN_HALO, FFN_CHUNK), 1)

    def conv(c0):
        up = jnp.dot(h, wup_ref[:, c0:c0 + FFN_CHUNK], preferred_element_type=F32)
        up = up.reshape(bb, tm, FFN_CHUNK)
        before = prev_ref[:, :, c0:c0 + FFN_CHUNK]
        tail = up[:, tm - FFN_HALO:tm, :]
        prev_ref[:, :, c0:c0 + FFN_CHUNK] = tail
        nf_ref[:, :, c0:c0 + FFN_CHUNK] = tail
        w = fdw_ref[:, c0:c0 + FFN_CHUNK]
        out = up * w[FFN_CONV_W - 1:FFN_CONV_W] + fdb_ref[:, c0:c0 + FFN_CHUNK]
        for back in range(1, FFN_CONV_W):
            moved = pltpu.roll(up, back, axis=1)
            head = jnp.where(first_rows < back, pltpu.roll(before, back, axis=1), moved[:, 0:FFN_HALO, :])
            moved = jnp.concatenate([head, moved[:, FFN_HALO:, :]], axis=1)
            out = out + moved * w[FFN_CONV_W - 1 - back:FFN_CONV_W - back]
        return out.reshape(rows, FFN_CHUNK)

    for c in range(D_FF // FFN_CHUNK):
        gate = conv(c * FFN_CHUNK)
        val = conv(D_FF + c * FFN_CHUNK)
        act_ref[:, c * FFN_CHUNK:(c + 1) * FFN_CHUNK] = (gate * _sigmoid(gate) * val).astype(BF16)

    y = x1 + jnp.dot(act_ref[...], wdn_ref[...], preferred_element_type=F32)
    if final_norm:
        y = y * lax.rsqrt(jnp.mean(y * y, axis=-1, keepdims=True) + EPS) * fg_ref[...]
    y_ref[...] = y.reshape(bb, tm, D_MODEL)


def _ffn(x1, state, g2, wup, fdw, fdb, wdn, fg, *, layer, bb, tm, final_norm):
    nb, t, _ = x1.shape
    grid = (nb // bb, t // tm)
    row = pl.BlockSpec((bb, tm, D_MODEL), lambda b, i: (b, i, 0))
    st_spec = pl.BlockSpec((bb, FFN_HALO, 2 * D_FF), lambda b, i: (b, 0, 0))
    return pl.pallas_call(
        functools.partial(_ffn_kernel, final_norm=final_norm),
        grid=grid,
        in_specs=[row, st_spec, _layer_spec((1, D_MODEL), layer), _layer_spec((D_MODEL, 2 * D_FF), layer),
                  _layer_spec((FFN_CONV_W, 2 * D_FF), layer), _layer_spec((1, 2 * D_FF), layer),
                  _layer_spec((D_FF, D_MODEL), layer), _const_spec((1, D_MODEL))],
        out_specs=[row, st_spec],
        out_shape=[jax.ShapeDtypeStruct((nb, t, D_MODEL), F32),
                   jax.ShapeDtypeStruct((nb, FFN_HALO, 2 * D_FF), F32)],
        scratch_shapes=[pltpu.VMEM((bb, FFN_HALO, 2 * D_FF), F32),
                        pltpu.VMEM((bb * tm, D_FF), BF16)],
        compiler_params=_compiler_params(),
        name="ffn",
    )(x1, state, g2, wup, fdw, fdb, wdn, fg)


def _cumsum_matrix():
    j = np.arange(2 * KEY_BLOCK)[:, None]
    s = np.arange(2 * KEY_BLOCK)[None, :]
    same = (j // KEY_BLOCK) == (s // KEY_BLOCK)
    return (same & ((j % KEY_BLOCK) > (s % KEY_BLOCK))).astype(np.float32)


def _head_mean_matrix():
    h = np.arange(ATTN_W) // HEAD_DIM
    return (h[:, None] == h[None, :]).astype(np.float32)


def _pad_rows_front(a, rows):
    return jnp.pad(a, ((0, 0), (rows - a.shape[1], 0), (0, 0)))


def kernel(x_prompt, x_sample, cache_sb_k, cache_sb_v, state_conv, state_ffn_conv, norm1_g, w_in, b_in,
           q_norm_g, k_norm_g, w_attn_out, dw_w, dw_b, cn_g, cn_b, w_conv_out, b_conv_out, w_out,
           norm2_g, w_up, ffn_dw_w, ffn_dw_b, w_down, final_g):
    depth = w_in.shape[0]
    bp, seq, _ = x_prompt.shape
    bs, dec, _ = x_sample.shape
    past = cache_sb_k.shape[2]
    assert all(seq % tile == 0 for tile in PROMPT_TILES[1:]) and past % KEY_BLOCK == 0
    assert CONV_W - 1 <= dec <= KEY_BLOCK and dec % (2 * SUBLANES) == 0
    um = jnp.asarray(_cumsum_matrix(), BF16)
    hm = jnp.asarray(_head_mean_matrix(), BF16)

    feature_major = lambda c: jnp.transpose(c, (0, 1, 3, 4, 2)).reshape(depth, bs, ATTN_W, past)
    kt_cache, vt_cache = feature_major(cache_sb_k), feature_major(cache_sb_v)
    conv_zero = jnp.zeros((bp, CONV_HALO, CONV_CH), F32)
    ffn_zero = jnp.zeros((bp, FFN_HALO, 2 * D_FF), F32)

    vec = lambda a: a.reshape(depth, 1, -1)
    per_head = lambda g: vec(jnp.tile(g, (1, N_HEADS)))
    proj_params = (vec(norm1_g), w_in.astype(BF16), vec(b_in), per_head(q_norm_g), per_head(k_norm_g))
    mix_params = (jnp.broadcast_to(dw_w[:, :, None, :], (depth, CONV_W, SUBLANES, CONV_CH)), vec(dw_b), vec(cn_g),
                  vec(cn_b), w_conv_out.astype(BF16), vec(b_conv_out), w_attn_out.astype(BF16), w_out.astype(BF16))
    ffn_params = (vec(norm2_g), w_up.astype(BF16), ffn_dw_w, vec(ffn_dw_b), w_down.astype(BF16))

    def run(l, x, conv_state, ffn_state, tiles, kv_stack=None):
        bb = tiles.batch
        q, k, v, kbt, vbd, u, sga, sgc = _inproj(x, *proj_params, hm, layer=l, bb=bb, tm=tiles.proj,
                                                 kv_stack=kv_stack)
        t = x.shape[1]
        if kv_stack is None:
            a = _attention(q, kbt, vbd, um, tq=tiles.attn, kt_cache=kt_cache, vt_cache=vt_cache, layer=l)
        else:
            a = _attention(q, kbt, vbd, um, tq=tiles.attn)
        x1 = _mix(x, a, u, conv_state, sga, sgc, *mix_params, layer=l, bb=bb, tm=tiles.mix)
        y, nf = _ffn(x1, ffn_state, *ffn_params, final_g.reshape(1, -1), layer=l, bb=bb, tm=tiles.ffn,
                     final_norm=l == depth - 1)
        return y, k, v, u[:, t - (CONV_W - 1):, :], nf[:, FFN_HALO - (FFN_CONV_W - 1):, :]

    xp, xs = x_prompt, x_sample
    kp, vp = jnp.zeros((depth, bp, ATTN_W, seq), F32), jnp.zeros((depth, bp, ATTN_W, seq), F32)
    outs = {name: [] for name in ("ks", "vs", "cp", "cs", "fp", "fs")}
    for l in range(depth):
        xp, kp, vp, c1, f1 = run(l, xp, conv_zero, ffn_zero, PROMPT_TILES, kv_stack=(kp, vp))
        xs, k2, v2, c2, f2 = run(l, xs, _pad_rows_front(state_conv[l], CONV_HALO),
                                 _pad_rows_front(state_ffn_conv[l], FFN_HALO),
                                 Tiles(batch=bs, proj=dec, attn=dec, mix=dec, ffn=dec))
        for name, val in zip(("ks", "vs", "cp", "cs", "fp", "fs"), (k2, v2, c1, c2, f1, f2)):
            outs[name].append(val)

    heads = lambda lst, b, t: jnp.stack(lst).reshape(depth, b, t, N_HEADS, HEAD_DIM)
    heads_t = lambda a: jnp.transpose(a.reshape(depth, bp, N_HEADS, HEAD_DIM, seq), (0, 1, 4, 2, 3))
    return (xp, xs, heads_t(kp), heads_t(vp), heads(outs["ks"], bs, dec), heads(outs["vs"], bs, dec),
            jnp.stack(outs["cp"]), jnp.stack(outs["cs"]), jnp.stack(outs["fp"]), jnp.stack(outs["fs"]))
```

```python
import functools
from typing import NamedTuple

import numpy as np
import jax
import jax.numpy as jnp
from jax import lax
from jax.experimental import pallas as pl
from jax.experimental.pallas import tpu as pltpu

D_MODEL = 1024
N_HEADS = 8
HEAD_DIM = 64
ATTN_W = N_HEADS * HEAD_DIM
CONV_CH = 512
CONV_W = 31
D_FF = 2816
FFN_CONV_W = 3
N_IN = 3 * ATTN_W + 2 * CONV_CH + 2 * D_MODEL
EPS = 1e-6

LANES = 128
SUBLANES = 8
KEY_BLOCK = 128
HEAD_PAIRS = ATTN_W // LANES
CONV_HALO = 32
CONV_ROWS = 32
FFN_HALO = SUBLANES
FFN_CHUNK = 256


class Tiles(NamedTuple):
    batch: int
    proj: int
    attn: int
    mix: int
    ffn: int


PROMPT_TILES = Tiles(batch=1, proj=512, attn=512, mix=256, ffn=256)
VMEM_LIMIT = 56 * 1024 * 1024
LOG2E = 1.4426950408889634
CARRY_LIMIT = 152.0

F32 = jnp.float32
BF16 = jnp.bfloat16


def _const_spec(shape):
    return pl.BlockSpec(shape, lambda *_: (0,) * len(shape), pipeline_mode=pl.Buffered(1))


def _compiler_params():
    return pltpu.CompilerParams(dimension_semantics=("arbitrary", "arbitrary"), vmem_limit_bytes=VMEM_LIMIT)


def _layer_spec(shape, layer):
    return pl.BlockSpec((None,) + tuple(shape), lambda *_: (layer,) + (0,) * len(shape),
                        pipeline_mode=pl.Buffered(1))


def _sigmoid(x):
    return 1.0 / (1.0 + jnp.exp2(x * -LOG2E))


def _inproj_kernel(x_ref, g1_ref, win_ref, bin_ref, qg_ref, kg_ref, hm_ref, *refs, feature_major):
    q_ref, k_ref, v_ref, kbt_ref, vbd_ref, u_ref, sga_ref, sgc_ref = refs[-8:]
    bb, tm, _ = x_ref.shape
    rows = bb * tm
    x = x_ref[...].reshape(rows, D_MODEL)
    h = x * lax.rsqrt(jnp.mean(x * x, axis=-1, keepdims=True) + EPS) * g1_ref[...]
    h = h.astype(BF16)

    def proj(c0, width):
        return (jnp.dot(h, win_ref[:, c0:c0 + width], preferred_element_type=F32)
                + bin_ref[:, c0:c0 + width])

    def head_rmsnorm(y, g):
        ssum = jnp.dot((y * y).astype(BF16), hm_ref[...], preferred_element_type=F32)
        return y * lax.rsqrt(ssum * (1.0 / HEAD_DIM) + EPS) * g

    def head_rmsnorm_t(y, g):
        per_head = y.T.reshape(N_HEADS, HEAD_DIM, y.shape[0])
        scale = lax.rsqrt(jnp.mean(per_head * per_head, axis=1, keepdims=True) + EPS)
        return (per_head * scale).reshape(ATTN_W, y.shape[0]) * g

    v = proj(2 * ATTN_W, ATTN_W)
    if feature_major:
        assert bb == 1 and tm % KEY_BLOCK == 0
        q = head_rmsnorm_t(proj(0, ATTN_W), qg_ref[...] * (HEAD_DIM ** -0.5 * LOG2E)).T
        q_ref[...] = q.astype(BF16).reshape(bb, tm, ATTN_W)
        k_t = head_rmsnorm_t(proj(ATTN_W, ATTN_W), kg_ref[...])
        k_ref[0] = k_t
        v_ref[0] = v.T
    else:
        q = head_rmsnorm(proj(0, ATTN_W), qg_ref[...])
        q_ref[...] = (q * (HEAD_DIM ** -0.5 * LOG2E)).astype(BF16).reshape(bb, tm, ATTN_W)
        k = head_rmsnorm(proj(ATTN_W, ATTN_W), kg_ref[...])
        k_ref[...] = k.reshape(bb, tm, ATTN_W)
        v_ref[...] = v.reshape(bb, tm, ATTN_W)

    blk = min(tm, KEY_BLOCK)
    even_row = (lax.broadcasted_iota(jnp.int32, (ATTN_W, KEY_BLOCK), 0) & (LANES - 1)) < HEAD_DIM
    even_lane = (lax.broadcasted_iota(jnp.int32, (KEY_BLOCK, ATTN_W), 1) & (LANES - 1)) < HEAD_DIM
    for b in range(bb):
        for c in range(tm // blk):
            r0 = b * tm + c * blk
            vb = v[r0:r0 + blk]
            if feature_major:
                kt = k_t[:, r0:r0 + blk]
            else:
                kb = k[r0:r0 + blk]
                if blk < KEY_BLOCK:
                    fill = jnp.zeros((KEY_BLOCK - blk, ATTN_W), F32)
                    kb, vb = jnp.concatenate([kb, fill], axis=0), jnp.concatenate([vb, fill], axis=0)
                kt = kb.T
            o = 2 * c * KEY_BLOCK
            kbt_ref[b, :, o:o + KEY_BLOCK] = jnp.where(even_row, kt, 0.0).astype(BF16)
            kbt_ref[b, :, o + KEY_BLOCK:o + 2 * KEY_BLOCK] = jnp.where(even_row, 0.0, kt).astype(BF16)
            vbd_ref[b, o:o + KEY_BLOCK, :] = jnp.where(even_lane, vb, 0.0).astype(BF16)
            vbd_ref[b, o + KEY_BLOCK:o + 2 * KEY_BLOCK, :] = jnp.where(even_lane, 0.0, vb).astype(BF16)

    c0 = 3 * ATTN_W
    u = proj(c0, CONV_CH) * _sigmoid(proj(c0 + CONV_CH, CONV_CH))
    u_ref[...] = u.reshape(bb, tm, CONV_CH)
    c0 += 2 * CONV_CH
    sga_ref[...] = _sigmoid(proj(c0, D_MODEL)).astype(BF16).reshape(bb, tm, D_MODEL)
    sgc_ref[...] = _sigmoid(proj(c0 + D_MODEL, D_MODEL)).astype(BF16).reshape(bb, tm, D_MODEL)


def _inproj(x, g1, win, b_in, qg, kg, hm, *, layer, bb, tm, kv_stack=None):
    nb, t, _ = x.shape
    grid = (nb // bb, t // tm)
    row = lambda width: pl.BlockSpec((bb, tm, width), lambda b, i: (b, i, 0))
    shp = lambda width, dt: jax.ShapeDtypeStruct((nb, t, width), dt)
    kw = 2 * KEY_BLOCK * (tm // min(tm, KEY_BLOCK))
    in_specs = [row(D_MODEL), _layer_spec((1, D_MODEL), layer), _layer_spec((D_MODEL, N_IN), layer),
                _layer_spec((1, N_IN), layer), _layer_spec((1, ATTN_W), layer),
                _layer_spec((1, ATTN_W), layer), _const_spec((ATTN_W, ATTN_W))]
    args = [x, g1, win, b_in, qg, kg, hm]
    aliases = {}
    if kv_stack is not None:
        for n, gain in ((4, qg), (5, kg)):
            in_specs[n], args[n] = _layer_spec((ATTN_W, 1), layer), gain.reshape(-1, ATTN_W, 1)
        kv_spec = pl.BlockSpec((None, bb, ATTN_W, tm), lambda b, i: (layer, b, 0, i))
        kv_shape = jax.ShapeDtypeStruct(kv_stack[0].shape, F32)
        aliases = {len(args): 1, len(args) + 1: 2}
        in_specs += [pl.BlockSpec(memory_space=pl.ANY)] * 2
        args += list(kv_stack)
    else:
        kv_spec, kv_shape = row(ATTN_W), shp(ATTN_W, F32)
    return pl.pallas_call(
        functools.partial(_inproj_kernel, feature_major=kv_stack is not None),
        grid=grid,
        in_specs=in_specs,
        input_output_aliases=aliases,
        out_specs=[row(ATTN_W), kv_spec, kv_spec,
                   pl.BlockSpec((bb, ATTN_W, kw), lambda b, i: (b, 0, i)),
                   pl.BlockSpec((bb, kw, ATTN_W), lambda b, i: (b, i, 0)),
                   row(CONV_CH), row(D_MODEL), row(D_MODEL)],
        out_shape=[shp(ATTN_W, BF16), kv_shape, kv_shape,
                   jax.ShapeDtypeStruct((nb, ATTN_W, kw * (t // tm)), BF16),
                   jax.ShapeDtypeStruct((nb, kw * (t // tm), ATTN_W), BF16),
                   shp(CONV_CH, F32), shp(D_MODEL, BF16), shp(D_MODEL, BF16)],
        compiler_params=_compiler_params(),
        name="inproj",
    )(*args)


def _attn_kernel(*refs, tq, n_past, layer):
    if n_past:
        (q_ref, kn_ref, vn_ref, um_ref, kp_ref, vp_ref, o_ref,
         carry_ref, acc_ref, sp_ref, zs_ref, kbuf_ref, vbuf_ref, sem_ref) = refs
    else:
        q_ref, kn_ref, vn_ref, um_ref, o_ref, carry_ref, acc_ref, sp_ref, zs_ref = refs
    batch = pl.program_id(0)
    i = pl.program_id(1)
    chains, sub = carry_ref.shape[0], carry_ref.shape[2]
    carry_ref[...] = jnp.zeros_like(carry_ref)
    acc_ref[...] = jnp.zeros_like(acc_ref)

    def step(c, blocks, key0=None):
        rows = slice(c * sub, (c + 1) * sub)
        if key0 is None:
            mask = None
        else:
            qpos = i * tq + c * sub + lax.broadcasted_iota(jnp.int32, (sub, 2 * KEY_BLOCK), 0)
            col = lax.broadcasted_iota(jnp.int32, (sub, 2 * KEY_BLOCK), 1) & (KEY_BLOCK - 1)
            mask = key0 + col < qpos
        part = lambda j, p: slice((j * HEAD_PAIRS + p) * sub, (j * HEAD_PAIRS + p + 1) * sub)
        for j, (scores, _) in enumerate(blocks):
            for p in range(HEAD_PAIRS):
                z = scores(p, rows)
                sp = jnp.where(z > 64.0, z, jnp.log(1.0 + jnp.exp2(z)) * LOG2E)
                if mask is not None:
                    sp = jnp.where(mask, sp, 0.0)
                sp_ref[c, part(j, p), :] = sp.astype(BF16)
                zs_ref[c, part(j, p), :] = z - sp
        later_all = jnp.dot(sp_ref[c, 0:len(blocks) * HEAD_PAIRS * sub, :], um_ref[...],
                            preferred_element_type=F32)
        for j, (_, weighted) in enumerate(blocks):
            smallest = None
            for p in range(HEAD_PAIRS):
                later = later_all[part(j, p)]
                carry = carry_ref[c, p]
                w = jnp.exp2(zs_ref[c, part(j, p), :] - later - carry)
                if mask is not None:
                    w = jnp.where(mask, w, 0.0)
                acc_ref[c, p] += weighted(p, w.astype(BF16))
                halves = []
                for c0 in (0, KEY_BLOCK):
                    own = sp_ref[c, part(j, p), c0:c0 + KEY_BLOCK].astype(F32)
                    halves.append(jnp.broadcast_to(later[:, c0:c0 + 1] + own[:, 0:1], (sub, KEY_BLOCK)))
                carry = carry + jnp.concatenate(halves, axis=1)
                carry_ref[c, p] = carry
                smallest = carry if smallest is None else jnp.minimum(smallest, carry)
        return jnp.min(smallest)

    def new_block(jb):
        start = pl.multiple_of(jb * 2 * KEY_BLOCK, 2 * KEY_BLOCK)
        scores = lambda p, rows: jnp.dot(q_ref[0, rows, p * LANES:(p + 1) * LANES],
                                         kn_ref[0, p * LANES:(p + 1) * LANES, pl.ds(start, 2 * KEY_BLOCK)],
                                         preferred_element_type=F32)
        weighted = lambda p, w: jnp.dot(w, vn_ref[0, pl.ds(start, 2 * KEY_BLOCK), p * LANES:(p + 1) * LANES],
                                        preferred_element_type=F32)
        return scores, weighted

    def past_block(slot):
        even_feature = lax.broadcasted_iota(jnp.int32, (LANES, KEY_BLOCK), 0) < HEAD_DIM

        def split(buf_ref, p):
            blk = buf_ref[slot, p * LANES:(p + 1) * LANES, :]
            return jnp.concatenate([jnp.where(even_feature, blk, 0.0).astype(BF16),
                                    jnp.where(even_feature, 0.0, blk).astype(BF16)], axis=1)

        scores = lambda p, rows: jnp.dot(q_ref[0, rows, p * LANES:(p + 1) * LANES], split(kbuf_ref, p),
                                         preferred_element_type=F32)
        weighted = lambda p, w: lax.dot_general(w, split(vbuf_ref, p), (((1,), (1,)), ((), ())),
                                                preferred_element_type=F32)
        return scores, weighted

    def past_copies(n, slot):
        start = pl.multiple_of((n_past - 1 - n) * KEY_BLOCK, KEY_BLOCK)
        return [pltpu.make_async_copy(src.at[layer, batch, :, pl.ds(start, KEY_BLOCK)], dst.at[slot],
                                      sem_ref.at[which, slot])
                for which, (src, dst) in enumerate(((kp_ref, kbuf_ref), (vp_ref, vbuf_ref)))]

    n_open = [(i * tq + c * sub) // KEY_BLOCK for c in range(chains)]

    def unfinished(n_blocks):
        return lambda state: jnp.logical_and(state[0] < n_blocks, state[1] < CARRY_LIMIT)

    smallest = [step(c, [new_block(n_open[c])], key0=n_open[c] * KEY_BLOCK) for c in range(chains)]
    if chains > 1:
        older = lambda c, k: [new_block(n_open[c] - 1 - k), new_block(n_open[c] - 2 - k)]
        k_all, *smallest = lax.while_loop(
            lambda state: jnp.logical_and(state[0] + 2 <= n_open[0],
                                          functools.reduce(jnp.minimum, state[1:]) < CARRY_LIMIT),
            lambda state: (state[0] + 2, *[step(c, older(c, state[0])) for c in range(chains)]),
            (jnp.int32(0), *smallest))
        for c in range(1, chains):
            lax.while_loop(unfinished(n_open[c]),
                           lambda state, c=c: (state[0] + 1, step(c, [new_block(n_open[c] - 1 - state[0])])),
                           (k_all, smallest[c]))
    else:
        _, smallest = lax.while_loop(
            unfinished(n_open[0]),
            lambda state: (state[0] + 1, step(0, [new_block(n_open[0] - 1 - state[0])])),
            (jnp.int32(0), smallest[0]))

    if n_past:
        for copy in past_copies(0, 0):
            copy.start()

        def past_body(state):
            n = state[0]
            slot = n & 1
            for copy in past_copies(n, slot):
                copy.wait()

            @pl.when(n + 1 < n_past)
            def _():
                for copy in past_copies(n + 1, 1 - slot):
                    copy.start()

            return n + 1, step(0, [past_block(slot)])

        n_used, _ = lax.while_loop(unfinished(n_past), past_body, (jnp.int32(0), smallest))

        @pl.when(n_used < n_past)
        def _():
            for copy in past_copies(n_used, n_used & 1):
                copy.wait()

    for c in range(chains):
        for p in range(HEAD_PAIRS):
            o_ref[0, c * sub:(c + 1) * sub, p * LANES:(p + 1) * LANES] = acc_ref[c, p].astype(BF16)


def _attention(q, kbt_new, vbd_new, um, *, tq, kt_cache=None, vt_cache=None, layer=0):
    nb, t, _ = q.shape
    tn2 = vbd_new.shape[1]
    n_past = 0 if kt_cache is None else kt_cache.shape[3] // KEY_BLOCK
    chains = max(1, tq // KEY_BLOCK)
    sub = tq // chains
    assert tq == chains * sub and (chains == 1 or (sub == KEY_BLOCK and chains % 2 == 0))
    grid = (nb, t // tq)
    q_spec = pl.BlockSpec((1, tq, ATTN_W), lambda b, i: (b, i, 0))
    in_specs = [q_spec, pl.BlockSpec((1, ATTN_W, tn2), lambda b, i: (b, 0, 0)),
                pl.BlockSpec((1, tn2, ATTN_W), lambda b, i: (b, 0, 0)), _const_spec(um.shape)]
    args = [q, kbt_new, vbd_new, um]
    scratch = [pltpu.VMEM((chains, HEAD_PAIRS, sub, 2 * KEY_BLOCK), F32),
               pltpu.VMEM((chains, HEAD_PAIRS, sub, LANES), F32),
               pltpu.VMEM((chains, 2 * HEAD_PAIRS * sub, 2 * KEY_BLOCK), BF16),
               pltpu.VMEM((chains, 2 * HEAD_PAIRS * sub, 2 * KEY_BLOCK), F32)]
    if n_past:
        assert chains == 1 and kt_cache.shape[3] % KEY_BLOCK == 0 and kt_cache.shape[1:3] == (nb, ATTN_W)
        in_specs += [pl.BlockSpec(memory_space=pl.ANY)] * 2
        args += [kt_cache, vt_cache]
        scratch += [pltpu.VMEM((2, ATTN_W, KEY_BLOCK), F32), pltpu.VMEM((2, ATTN_W, KEY_BLOCK), F32),
                    pltpu.SemaphoreType.DMA((2, 2))]
    return pl.pallas_call(
        functools.partial(_attn_kernel, tq=tq, n_past=n_past, layer=layer),
        grid=grid,
        in_specs=in_specs,
        out_specs=q_spec,
        out_shape=jax.ShapeDtypeStruct((nb, t, ATTN_W), BF16),
        scratch_shapes=scratch,
        compiler_params=_compiler_params(),
        name="attention",
    )(*args)


def _mix_kernel(x_ref, a_ref, u_ref, uh_ref, st_ref, sga_ref, sgc_ref, dww_ref, dwb_ref, cng_ref,
                cnb_ref, wco_ref, bco_ref, wao_ref, wo_ref, x1_ref, ext_ref, act_ref):
    i = pl.program_id(1)
    bb, tm, _ = x_ref.shape
    rows = bb * tm

    @pl.when(i == 0)
    def _():
        ext_ref[0, :, 0:CONV_HALO, :] = st_ref[...]

    @pl.when(i > 0)
    def _():
        ext_ref[0, :, 0:CONV_HALO, :] = uh_ref[...]

    ext_ref[0, :, CONV_HALO:CONV_HALO + tm, :] = u_ref[...]
    span = CONV_HALO + tm - SUBLANES
    for s in range(1, SUBLANES):
        ext_ref[s, :, 0:span, :] = ext_ref[0, :, s:s + span, :]

    chunk = CONV_ROWS
    first = CONV_HALO - (CONV_W - 1)

    def conv_rows(b, r0):
        acc = jnp.broadcast_to(dwb_ref[...], (chunk // SUBLANES, SUBLANES, CONV_CH))
        for j in range(CONV_W):
            s, base = (first + j) % SUBLANES, (first + j) // SUBLANES * SUBLANES
            taps = ext_ref[s, b, r0 + base:r0 + base + chunk, :]
            acc = acc + taps.reshape(chunk // SUBLANES, SUBLANES, CONV_CH) * dww_ref[j]
        acc = acc.reshape(chunk, CONV_CH)
        mu = jnp.mean(acc, axis=-1, keepdims=True)
        xc = acc - mu
        y = xc * lax.rsqrt(jnp.mean(xc * xc, axis=-1, keepdims=True) + EPS)
        y = y * cng_ref[...] + cnb_ref[...]
        act_ref[b * tm + r0:b * tm + r0 + chunk, :] = (y * _sigmoid(y)).astype(BF16)

    for b in range(bb):
        for r0 in range(0, tm, chunk):
            conv_rows(b, r0)
    c = jnp.dot(act_ref[...], wco_ref[...], preferred_element_type=F32) + bco_ref[...]
    a = jnp.dot(a_ref[...].reshape(rows, ATTN_W), wao_ref[...], preferred_element_type=F32)
    m = (sga_ref[...].reshape(rows, D_MODEL).astype(F32) * a
         + sgc_ref[...].reshape(rows, D_MODEL).astype(F32) * c)
    x1 = x_ref[...].reshape(rows, D_MODEL) + jnp.dot(m.astype(BF16), wo_ref[...],
                                                     preferred_element_type=F32)
    x1_ref[...] = x1.reshape(bb, tm, D_MODEL)


def _mix(x, a, u, state, sga, sgc, dww, dwb, cng, cnb, wco, bco, wao, wo, *, layer, bb, tm):
    nb, t, _ = x.shape
    grid = (nb // bb, t // tm)
    row = lambda width: pl.BlockSpec((bb, tm, width), lambda b, i: (b, i, 0))
    halo_blocks = tm // CONV_HALO
    halo = pl.BlockSpec((bb, CONV_HALO, CONV_CH),
                        lambda b, i: (b, jnp.maximum(i * halo_blocks - 1, 0), 0))
    st_spec = pl.BlockSpec((bb, CONV_HALO, CONV_CH), lambda b, i: (b, 0, 0))
    return pl.pallas_call(
        _mix_kernel,
        grid=grid,
        in_specs=[row(D_MODEL), row(ATTN_W), row(CONV_CH), halo, st_spec, row(D_MODEL), row(D_MODEL),
                  _layer_spec((CONV_W, SUBLANES, CONV_CH), layer), _layer_spec((1, CONV_CH), layer),
                  _layer_spec((1, CONV_CH), layer), _layer_spec((1, CONV_CH), layer),
                  _layer_spec((CONV_CH, D_MODEL), layer), _layer_spec((1, D_MODEL), layer),
                  _layer_spec((ATTN_W, D_MODEL), layer), _layer_spec((D_MODEL, D_MODEL), layer)],
        out_specs=row(D_MODEL),
        out_shape=jax.ShapeDtypeStruct((nb, t, D_MODEL), F32),
        scratch_shapes=[pltpu.VMEM((SUBLANES, bb, CONV_HALO + tm, CONV_CH), F32),
                        pltpu.VMEM((bb * tm, CONV_CH), BF16)],
        compiler_params=_compiler_params(),
        name="mix",
    )(x, a, u, u, state, sga, sgc, dww, dwb, cng, cnb, wco, bco, wao, wo)


def _ffn_kernel(x1_ref, st_ref, g2_ref, wup_ref, fdw_ref, fdb_ref, wdn_ref, fg_ref,
                y_ref, nf_ref, prev_ref, act_ref, *, final_norm):
    i = pl.program_id(1)
    bb, tm, _ = x1_ref.shape
    rows = bb * tm

    @pl.when(i == 0)
    def _():
        prev_ref[...] = st_ref[...]

    x1 = x1_ref[...].reshape(rows, D_MODEL)
    h = x1 * lax.rsqrt(jnp.mean(x1 * x1, axis=-1, keepdims=True) + EPS) * g2_ref[...]
    h = h.astype(BF16)

    first_rows = lax.broadcasted_iota(jnp.int32, (bb, FFN_HALO, FFN_CHUNK), 1)

    def conv(c0):
        up = jnp.dot(h, wup_ref[:, c0:c0 + FFN_CHUNK], preferred_element_type=F32)
        up = up.reshape(bb, tm, FFN_CHUNK)
        before = prev_ref[:, :, c0:c0 + FFN_CHUNK]
        tail = up[:, tm - FFN_HALO:tm, :]
        prev_ref[:, :, c0:c0 + FFN_CHUNK] = tail
        nf_ref[:, :, c0:c0 + FFN_CHUNK] = tail
        w = fdw_ref[:, c0:c0 + FFN_CHUNK]
        out = up * w[FFN_CONV_W - 1:FFN_CONV_W] + fdb_ref[:, c0:c0 + FFN_CHUNK]
        for back in range(1, FFN_CONV_W):
            moved = pltpu.roll(up, back, axis=1)
            head = jnp.where(first_rows < back, pltpu.roll(before, back, axis=1), moved[:, 0:FFN_HALO, :])
            moved = jnp.concatenate([head, moved[:, FFN_HALO:, :]], axis=1)
            out = out + moved * w[FFN_CONV_W - 1 - back:FFN_CONV_W - back]
        return out.reshape(rows, FFN_CHUNK)

    for c in range(D_FF // FFN_CHUNK):
        gate = conv(c * FFN_CHUNK)
        val = conv(D_FF + c * FFN_CHUNK)
        act_ref[:, c * FFN_CHUNK:(c + 1) * FFN_CHUNK] = (gate * _sigmoid(gate) * val).astype(BF16)

    y = x1 + jnp.dot(act_ref[...], wdn_ref[...], preferred_element_type=F32)
    if final_norm:
        y = y * lax.rsqrt(jnp.mean(y * y, axis=-1, keepdims=True) + EPS) * fg_ref[...]
    y_ref[...] = y.reshape(bb, tm, D_MODEL)


def _ffn(x1, state, g2, wup, fdw, fdb, wdn, fg, *, layer, bb, tm, final_norm):
    nb, t, _ = x1.shape
    grid = (nb // bb, t // tm)
    row = pl.BlockSpec((bb, tm, D_MODEL), lambda b, i: (b, i, 0))
    st_spec = pl.BlockSpec((bb, FFN_HALO, 2 * D_FF), lambda b, i: (b, 0, 0))
    return pl.pallas_call(
        functools.partial(_ffn_kernel, final_norm=final_norm),
        grid=grid,
        in_specs=[row, st_spec, _layer_spec((1, D_MODEL), layer), _layer_spec((D_MODEL, 2 * D_FF), layer),
                  _layer_spec((FFN_CONV_W, 2 * D_FF), layer), _layer_spec((1, 2 * D_FF), layer),
                  _layer_spec((D_FF, D_MODEL), layer), _const_spec((1, D_MODEL))],
        out_specs=[row, st_spec],
        out_shape=[jax.ShapeDtypeStruct((nb, t, D_MODEL), F32),
                   jax.ShapeDtypeStruct((nb, FFN_HALO, 2 * D_FF), F32)],
        scratch_shapes=[pltpu.VMEM((bb, FFN_HALO, 2 * D_FF), F32),
                        pltpu.VMEM((bb * tm, D_FF), BF16)],
        compiler_params=_compiler_params(),
        name="ffn",
    )(x1, state, g2, wup, fdw, fdb, wdn, fg)


def _cumsum_matrix():
    j = np.arange(2 * KEY_BLOCK)[:, None]
    s = np.arange(2 * KEY_BLOCK)[None, :]
    same = (j // KEY_BLOCK) == (s // KEY_BLOCK)
    return (same & ((j % KEY_BLOCK) > (s % KEY_BLOCK))).astype(np.float32)


def _head_mean_matrix():
    h = np.arange(ATTN_W) // HEAD_DIM
    return (h[:, None] == h[None, :]).astype(np.float32)


def _pad_rows_front(a, rows):
    return jnp.pad(a, ((0, 0), (rows - a.shape[1], 0), (0, 0)))


def kernel(x_prompt, x_sample, cache_sb_k, cache_sb_v, state_conv, state_ffn_conv, norm1_g, w_in, b_in,
           q_norm_g, k_norm_g, w_attn_out, dw_w, dw_b, cn_g, cn_b, w_conv_out, b_conv_out, w_out,
           norm2_g, w_up, ffn_dw_w, ffn_dw_b, w_down, final_g):
    depth = w_in.shape[0]
    bp, seq, _ = x_prompt.shape
    bs, dec, _ = x_sample.shape
    past = cache_sb_k.shape[2]
    assert all(seq % tile == 0 for tile in PROMPT_TILES[1:]) and past % KEY_BLOCK == 0
    assert CONV_W - 1 <= dec <= KEY_BLOCK and dec % (2 * SUBLANES) == 0
    um = jnp.asarray(_cumsum_matrix(), BF16)
    hm = jnp.asarray(_head_mean_matrix(), BF16)

    feature_major = lambda c: jnp.transpose(c, (0, 1, 3, 4, 2)).reshape(depth, bs, ATTN_W, past)
    kt_cache, vt_cache = feature_major(cache_sb_k), feature_major(cache_sb_v)
    conv_zero = jnp.zeros((bp, CONV_HALO, CONV_CH), F32)
    ffn_zero = jnp.zeros((bp, FFN_HALO, 2 * D_FF), F32)

    vec = lambda a: a.reshape(depth, 1, -1)
    per_head = lambda g: vec(jnp.tile(g, (1, N_HEADS)))
    proj_params = (vec(norm1_g), w_in.astype(BF16), vec(b_in), per_head(q_norm_g), per_head(k_norm_g))
    mix_params = (jnp.broadcast_to(dw_w[:, :, None, :], (depth, CONV_W, SUBLANES, CONV_CH)), vec(dw_b), vec(cn_g),
                  vec(cn_b), w_conv_out.astype(BF16), vec(b_conv_out), w_attn_out.astype(BF16), w_out.astype(BF16))
    ffn_params = (vec(norm2_g), w_up.astype(BF16), ffn_dw_w, vec(ffn_dw_b), w_down.astype(BF16))

    def run(l, x, conv_state, ffn_state, tiles, kv_stack=None):
        bb = tiles.batch
        q, k, v, kbt, vbd, u, sga, sgc = _inproj(x, *proj_params, hm, layer=l, bb=bb, tm=tiles.proj,
                                                 kv_stack=kv_stack)
        t = x.shape[1]
        if kv_stack is None:
            a = _attention(q, kbt, vbd, um, tq=tiles.attn, kt_cache=kt_cache, vt_cache=vt_cache, layer=l)
        else:
            a = _attention(q, kbt, vbd, um, tq=tiles.attn)
        x1 = _mix(x, a, u, conv_state, sga, sgc, *mix_params, layer=l, bb=bb, tm=tiles.mix)
        y, nf = _ffn(x1, ffn_state, *ffn_params, final_g.reshape(1, -1), layer=l, bb=bb, tm=tiles.ffn,
                     final_norm=l == depth - 1)
        return y, k, v, u[:, t - (CONV_W - 1):, :], nf[:, FFN_HALO - (FFN_CONV_W - 1):, :]

    xp, xs = x_prompt, x_sample
    kp, vp = jnp.zeros((depth, bp, ATTN_W, seq), F32), jnp.zeros((depth, bp, ATTN_W, seq), F32)
    outs = {name: [] for name in ("ks", "vs", "cp", "cs", "fp", "fs")}
    for l in range(depth):
        xp, kp, vp, c1, f1 = run(l, xp, conv_zero, ffn_zero, PROMPT_TILES, kv_stack=(kp, vp))
        xs, k2, v2, c2, f2 = run(l, xs, _pad_rows_front(state_conv[l], CONV_HALO),
                                 _pad_rows_front(state_ffn_conv[l], FFN_HALO),
                                 Tiles(batch=bs, proj=dec, attn=dec, mix=dec, ffn=dec))
        for name, val in zip(("ks", "vs", "cp", "cs", "fp", "fs"), (k2, v2, c1, c2, f1, f2)):
            outs[name].append(val)

    heads = lambda lst, b, t: jnp.stack(lst).reshape(depth, b, t, N_HEADS, HEAD_DIM)
    heads_t = lambda a: jnp.transpose(a.reshape(depth, bp, N_HEADS, HEAD_DIM, seq), (0, 1, 4, 2, 3))
    return (xp, xs, heads_t(kp), heads_t(vp), heads(outs["ks"], bs, dec), heads(outs["vs"], bs, dec),
            jnp.stack(outs["cp"]), jnp.stack(outs["cs"]), jnp.stack(outs["fp"]), jnp.stack(outs["fs"]))
```

```python
import functools
from typing import NamedTuple

import numpy as np
import jax
import jax.numpy as jnp
from jax import lax
from jax.experimental import pallas as pl
from jax.experimental.pallas import tpu as pltpu

D_MODEL = 1024
N_HEADS = 8
HEAD_DIM = 64
ATTN_W = N_HEADS * HEAD_DIM
CONV_CH = 512
CONV_W = 31
D_FF = 2816
FFN_CONV_W = 3
N_IN = 3 * ATTN_W + 2 * CONV_CH + 2 * D_MODEL
EPS = 1e-6

LANES = 128
SUBLANES = 8
KEY_BLOCK = 128
HEAD_PAIRS = ATTN_W // LANES
CONV_HALO = 32
CONV_ROWS = 32
FFN_HALO = SUBLANES
FFN_CHUNK = 256


class Tiles(NamedTuple):
    batch: int
    proj: int
    attn: int
    mix: int
    ffn: int


PROMPT_TILES = Tiles(batch=1, proj=512, attn=512, mix=512, ffn=256)
VMEM_LIMIT = 56 * 1024 * 1024
LOG2E = 1.4426950408889634
CARRY_LIMIT = 152.0

F32 = jnp.float32
BF16 = jnp.bfloat16


def _const_spec(shape):
    return pl.BlockSpec(shape, lambda *_: (0,) * len(shape), pipeline_mode=pl.Buffered(1))


def _compiler_params():
    return pltpu.CompilerParams(dimension_semantics=("arbitrary", "arbitrary"), vmem_limit_bytes=VMEM_LIMIT)


def _layer_spec(shape, layer):
    return pl.BlockSpec((None,) + tuple(shape), lambda *_: (layer,) + (0,) * len(shape),
                        pipeline_mode=pl.Buffered(1))


def _sigmoid(x):
    return 1.0 / (1.0 + jnp.exp2(x * -LOG2E))


def _inproj_kernel(x_ref, g1_ref, win_ref, bin_ref, qg_ref, kg_ref, hm_ref, *refs, feature_major):
    q_ref, k_ref, v_ref, kbt_ref, vbd_ref, u_ref, sga_ref, sgc_ref = refs[-8:]
    bb, tm, _ = x_ref.shape
    rows = bb * tm
    x = x_ref[...].reshape(rows, D_MODEL)
    h = x * lax.rsqrt(jnp.mean(x * x, axis=-1, keepdims=True) + EPS) * g1_ref[...]
    h = h.astype(BF16)

    def proj(c0, width):
        return (jnp.dot(h, win_ref[:, c0:c0 + width], preferred_element_type=F32)
                + bin_ref[:, c0:c0 + width])

    def head_rmsnorm(y, g):
        ssum = jnp.dot((y * y).astype(BF16), hm_ref[...], preferred_element_type=F32)
        return y * lax.rsqrt(ssum * (1.0 / HEAD_DIM) + EPS) * g

    def head_rmsnorm_t(y, g):
        per_head = y.T.reshape(N_HEADS, HEAD_DIM, y.shape[0])
        scale = lax.rsqrt(jnp.mean(per_head * per_head, axis=1, keepdims=True) + EPS)
        return (per_head * scale).reshape(ATTN_W, y.shape[0]) * g

    v = proj(2 * ATTN_W, ATTN_W)
    if feature_major:
        assert bb == 1 and tm % KEY_BLOCK == 0
        q = head_rmsnorm_t(proj(0, ATTN_W), qg_ref[...] * (HEAD_DIM ** -0.5 * LOG2E)).T
        q_ref[...] = q.astype(BF16).reshape(bb, tm, ATTN_W)
        k_t = head_rmsnorm_t(proj(ATTN_W, ATTN_W), kg_ref[...])
        k_ref[0] = k_t
        v_ref[0] = v.T
    else:
        q = head_rmsnorm(proj(0, ATTN_W), qg_ref[...])
        q_ref[...] = (q * (HEAD_DIM ** -0.5 * LOG2E)).astype(BF16).reshape(bb, tm, ATTN_W)
        k = head_rmsnorm(proj(ATTN_W, ATTN_W), kg_ref[...])
        k_ref[...] = k.reshape(bb, tm, ATTN_W)
        v_ref[...] = v.reshape(bb, tm, ATTN_W)

    blk = min(tm, KEY_BLOCK)
    even_row = (lax.broadcasted_iota(jnp.int32, (ATTN_W, KEY_BLOCK), 0) & (LANES - 1)) < HEAD_DIM
    even_lane = (lax.broadcasted_iota(jnp.int32, (KEY_BLOCK, ATTN_W), 1) & (LANES - 1)) < HEAD_DIM
    for b in range(bb):
        for c in range(tm // blk):
            r0 = b * tm + c * blk
            vb = v[r0:r0 + blk]
            if feature_major:
                kt = k_t[:, r0:r0 + blk]
            else:
                kb = k[r0:r0 + blk]
                if blk < KEY_BLOCK:
                    fill = jnp.zeros((KEY_BLOCK - blk, ATTN_W), F32)
                    kb, vb = jnp.concatenate([kb, fill], axis=0), jnp.concatenate([vb, fill], axis=0)
                kt = kb.T
            o = 2 * c * KEY_BLOCK
            kbt_ref[b, :, o:o + KEY_BLOCK] = jnp.where(even_row, kt, 0.0).astype(BF16)
            kbt_ref[b, :, o + KEY_BLOCK:o + 2 * KEY_BLOCK] = jnp.where(even_row, 0.0, kt).astype(BF16)
            vbd_ref[b, o:o + KEY_BLOCK, :] = jnp.where(even_lane, vb, 0.0).astype(BF16)
            vbd_ref[b, o + KEY_BLOCK:o + 2 * KEY_BLOCK, :] = jnp.where(even_lane, 0.0, vb).astype(BF16)

    c0 = 3 * ATTN_W
    u = proj(c0, CONV_CH) * _sigmoid(proj(c0 + CONV_CH, CONV_CH))
    u_ref[...] = u.reshape(bb, tm, CONV_CH)
    c0 += 2 * CONV_CH
    sga_ref[...] = _sigmoid(proj(c0, D_MODEL)).astype(BF16).reshape(bb, tm, D_MODEL)
    sgc_ref[...] = _sigmoid(proj(c0 + D_MODEL, D_MODEL)).astype(BF16).reshape(bb, tm, D_MODEL)


def _inproj(x, g1, win, b_in, qg, kg, hm, *, layer, bb, tm, kv_stack=None):
    nb, t, _ = x.shape
    grid = (nb // bb, t // tm)
    row = lambda width: pl.BlockSpec((bb, tm, width), lambda b, i: (b, i, 0))
    shp = lambda width, dt: jax.ShapeDtypeStruct((nb, t, width), dt)
    kw = 2 * KEY_BLOCK * (tm // min(tm, KEY_BLOCK))
    in_specs = [row(D_MODEL), _layer_spec((1, D_MODEL), layer), _layer_spec((D_MODEL, N_IN), layer),
                _layer_spec((1, N_IN), layer), _layer_spec((1, ATTN_W), layer),
                _layer_spec((1, ATTN_W), layer), _const_spec((ATTN_W, ATTN_W))]
    args = [x, g1, win, b_in, qg, kg, hm]
    aliases = {}
    if kv_stack is not None:
        for n, gain in ((4, qg), (5, kg)):
            in_specs[n], args[n] = _layer_spec((ATTN_W, 1), layer), gain.reshape(-1, ATTN_W, 1)
        kv_spec = pl.BlockSpec((None, bb, ATTN_W, tm), lambda b, i: (layer, b, 0, i))
        kv_shape = jax.ShapeDtypeStruct(kv_stack[0].shape, F32)
        aliases = {len(args): 1, len(args) + 1: 2}
        in_specs += [pl.BlockSpec(memory_space=pl.ANY)] * 2
        args += list(kv_stack)
    else:
        kv_spec, kv_shape = row(ATTN_W), shp(ATTN_W, F32)
    return pl.pallas_call(
        functools.partial(_inproj_kernel, feature_major=kv_stack is not None),
        grid=grid,
        in_specs=in_specs,
        input_output_aliases=aliases,
        out_specs=[row(ATTN_W), kv_spec, kv_spec,
                   pl.BlockSpec((bb, ATTN_W, kw), lambda b, i: (b, 0, i)),
                   pl.BlockSpec((bb, kw, ATTN_W), lambda b, i: (b, i, 0)),
                   row(CONV_CH), row(D_MODEL), row(D_MODEL)],
        out_shape=[shp(ATTN_W, BF16), kv_shape, kv_shape,
                   jax.ShapeDtypeStruct((nb, ATTN_W, kw * (t // tm)), BF16),
                   jax.ShapeDtypeStruct((nb, kw * (t // tm), ATTN_W), BF16),
                   shp(CONV_CH, F32), shp(D_MODEL, BF16), shp(D_MODEL, BF16)],
        compiler_params=_compiler_params(),
        name="inproj",
    )(*args)


def _attn_kernel(*refs, tq, n_past, layer):
    if n_past:
        (q_ref, kn_ref, vn_ref, um_ref, kp_ref, vp_ref, o_ref,
         carry_ref, acc_ref, sp_ref, zs_ref, kbuf_ref, vbuf_ref, sem_ref) = refs
    else:
        q_ref, kn_ref, vn_ref, um_ref, o_ref, carry_ref, acc_ref, sp_ref, zs_ref = refs
    batch = pl.program_id(0)
    i = pl.program_id(1)
    chains, sub = carry_ref.shape[0], carry_ref.shape[2]
    carry_ref[...] = jnp.zeros_like(carry_ref)
    acc_ref[...] = jnp.zeros_like(acc_ref)

    def step(c, blocks, key0=None):
        rows = slice(c * sub, (c + 1) * sub)
        if key0 is None:
            mask = None
        else:
            qpos = i * tq + c * sub + lax.broadcasted_iota(jnp.int32, (sub, 2 * KEY_BLOCK), 0)
            col = lax.broadcasted_iota(jnp.int32, (sub, 2 * KEY_BLOCK), 1) & (KEY_BLOCK - 1)
            mask = key0 + col < qpos
        part = lambda j, p: slice((j * HEAD_PAIRS + p) * sub, (j * HEAD_PAIRS + p + 1) * sub)
        for j, (scores, _) in enumerate(blocks):
            for p in range(HEAD_PAIRS):
                z = scores(p, rows)
                sp = jnp.where(z > 64.0, z, jnp.log(1.0 + jnp.exp2(z)) * LOG2E)
                if mask is not None:
                    sp = jnp.where(mask, sp, 0.0)
                sp_ref[c, part(j, p), :] = sp.astype(BF16)
                zs_ref[c, part(j, p), :] = z - sp
        later_all = jnp.dot(sp_ref[c, 0:len(blocks) * HEAD_PAIRS * sub, :], um_ref[...],
                            preferred_element_type=F32)
        for j, (_, weighted) in enumerate(blocks):
            smallest = None
            for p in range(HEAD_PAIRS):
                later = later_all[part(j, p)]
                carry = carry_ref[c, p]
                w = jnp.exp2(zs_ref[c, part(j, p), :] - later - carry)
                if mask is not None:
                    w = jnp.where(mask, w, 0.0)
                acc_ref[c, p] += weighted(p, w.astype(BF16))
                halves = []
                for c0 in (0, KEY_BLOCK):
                    own = sp_ref[c, part(j, p), c0:c0 + KEY_BLOCK].astype(F32)
                    halves.append(jnp.broadcast_to(later[:, c0:c0 + 1] + own[:, 0:1], (sub, KEY_BLOCK)))
                carry = carry + jnp.concatenate(halves, axis=1)
                carry_ref[c, p] = carry
                smallest = carry if smallest is None else jnp.minimum(smallest, carry)
        return jnp.min(smallest)

    def new_block(jb):
        start = pl.multiple_of(jb * 2 * KEY_BLOCK, 2 * KEY_BLOCK)
        scores = lambda p, rows: jnp.dot(q_ref[0, rows, p * LANES:(p + 1) * LANES],
                                         kn_ref[0, p * LANES:(p + 1) * LANES, pl.ds(start, 2 * KEY_BLOCK)],
                                         preferred_element_type=F32)
        weighted = lambda p, w: jnp.dot(w, vn_ref[0, pl.ds(start, 2 * KEY_BLOCK), p * LANES:(p + 1) * LANES],
                                        preferred_element_type=F32)
        return scores, weighted

    def past_block(slot):
        even_feature = lax.broadcasted_iota(jnp.int32, (LANES, KEY_BLOCK), 0) < HEAD_DIM

        def split(buf_ref, p):
            blk = buf_ref[slot, p * LANES:(p + 1) * LANES, :]
            return jnp.concatenate([jnp.where(even_feature, blk, 0.0).astype(BF16),
                                    jnp.where(even_feature, 0.0, blk).astype(BF16)], axis=1)

        scores = lambda p, rows: jnp.dot(q_ref[0, rows, p * LANES:(p + 1) * LANES], split(kbuf_ref, p),
                                         preferred_element_type=F32)
        weighted = lambda p, w: lax.dot_general(w, split(vbuf_ref, p), (((1,), (1,)), ((), ())),
                                                preferred_element_type=F32)
        return scores, weighted

    def past_copies(n, slot):
        start = pl.multiple_of((n_past - 1 - n) * KEY_BLOCK, KEY_BLOCK)
        return [pltpu.make_async_copy(src.at[layer, batch, :, pl.ds(start, KEY_BLOCK)], dst.at[slot],
                                      sem_ref.at[which, slot])
                for which, (src, dst) in enumerate(((kp_ref, kbuf_ref), (vp_ref, vbuf_ref)))]

    n_open = [(i * tq + c * sub) // KEY_BLOCK for c in range(chains)]

    def unfinished(n_blocks):
        return lambda state: jnp.logical_and(state[0] < n_blocks, state[1] < CARRY_LIMIT)

    smallest = [step(c, [new_block(n_open[c])], key0=n_open[c] * KEY_BLOCK) for c in range(chains)]
    if chains > 1:
        older = lambda c, k: [new_block(n_open[c] - 1 - k), new_block(n_open[c] - 2 - k)]
        k_all, *smallest = lax.while_loop(
            lambda state: jnp.logical_and(state[0] + 2 <= n_open[0],
                                          functools.reduce(jnp.minimum, state[1:]) < CARRY_LIMIT),
            lambda state: (state[0] + 2, *[step(c, older(c, state[0])) for c in range(chains)]),
            (jnp.int32(0), *smallest))
        for c in range(1, chains):
            lax.while_loop(unfinished(n_open[c]),
                           lambda state, c=c: (state[0] + 1, step(c, [new_block(n_open[c] - 1 - state[0])])),
                           (k_all, smallest[c]))
    else:
        _, smallest = lax.while_loop(
            unfinished(n_open[0]),
            lambda state: (state[0] + 1, step(0, [new_block(n_open[0] - 1 - state[0])])),
            (jnp.int32(0), smallest[0]))

    if n_past:
        for copy in past_copies(0, 0):
            copy.start()

        def past_body(state):
            n = state[0]
            slot = n & 1
            for copy in past_copies(n, slot):
                copy.wait()

            @pl.when(n + 1 < n_past)
            def _():
                for copy in past_copies(n + 1, 1 - slot):
                    copy.start()

            return n + 1, step(0, [past_block(slot)])

        n_used, _ = lax.while_loop(unfinished(n_past), past_body, (jnp.int32(0), smallest))

        @pl.when(n_used < n_past)
        def _():
            for copy in past_copies(n_used, n_used & 1):
                copy.wait()

    for c in range(chains):
        for p in range(HEAD_PAIRS):
            o_ref[0, c * sub:(c + 1) * sub, p * LANES:(p + 1) * LANES] = acc_ref[c, p].astype(BF16)


def _attention(q, kbt_new, vbd_new, um, *, tq, kt_cache=None, vt_cache=None, layer=0):
    nb, t, _ = q.shape
    tn2 = vbd_new.shape[1]
    n_past = 0 if kt_cache is None else kt_cache.shape[3] // KEY_BLOCK
    chains = max(1, tq // KEY_BLOCK)
    sub = tq // chains
    assert tq == chains * sub and (chains == 1 or (sub == KEY_BLOCK and chains % 2 == 0))
    grid = (nb, t // tq)
    q_spec = pl.BlockSpec((1, tq, ATTN_W), lambda b, i: (b, i, 0))
    in_specs = [q_spec, pl.BlockSpec((1, ATTN_W, tn2), lambda b, i: (b, 0, 0)),
                pl.BlockSpec((1, tn2, ATTN_W), lambda b, i: (b, 0, 0)), _const_spec(um.shape)]
    args = [q, kbt_new, vbd_new, um]
    scratch = [pltpu.VMEM((chains, HEAD_PAIRS, sub, 2 * KEY_BLOCK), F32),
               pltpu.VMEM((chains, HEAD_PAIRS, sub, LANES), F32),
               pltpu.VMEM((chains, 2 * HEAD_PAIRS * sub, 2 * KEY_BLOCK), BF16),
               pltpu.VMEM((chains, 2 * HEAD_PAIRS * sub, 2 * KEY_BLOCK), F32)]
    if n_past:
        assert chains == 1 and kt_cache.shape[3] % KEY_BLOCK == 0 and kt_cache.shape[1:3] == (nb, ATTN_W)
        in_specs += [pl.BlockSpec(memory_space=pl.ANY)] * 2
        args += [kt_cache, vt_cache]
        scratch += [pltpu.VMEM((2, ATTN_W, KEY_BLOCK), F32), pltpu.VMEM((2, ATTN_W, KEY_BLOCK), F32),
                    pltpu.SemaphoreType.DMA((2, 2))]
    return pl.pallas_call(
        functools.partial(_attn_kernel, tq=tq, n_past=n_past, layer=layer),
        grid=grid,
        in_specs=in_specs,
        out_specs=q_spec,
        out_shape=jax.ShapeDtypeStruct((nb, t, ATTN_W), BF16),
        scratch_shapes=scratch,
        compiler_params=_compiler_params(),
        name="attention",
    )(*args)


def _mix_kernel(x_ref, a_ref, u_ref, uh_ref, st_ref, sga_ref, sgc_ref, dww_ref, dwb_ref, cng_ref,
                cnb_ref, wco_ref, bco_ref, wao_ref, wo_ref, x1_ref, ext_ref, act_ref):
    i = pl.program_id(1)
    bb, tm, _ = x_ref.shape
    rows = bb * tm

    @pl.when(i == 0)
    def _():
        ext_ref[0, :, 0:CONV_HALO, :] = st_ref[...]

    @pl.when(i > 0)
    def _():
        ext_ref[0, :, 0:CONV_HALO, :] = uh_ref[...]

    ext_ref[0, :, CONV_HALO:CONV_HALO + tm, :] = u_ref[...]
    span = CONV_HALO + tm - SUBLANES
    for s in range(1, SUBLANES):
        ext_ref[s, :, 0:span, :] = ext_ref[0, :, s:s + span, :]

    chunk = CONV_ROWS
    first = CONV_HALO - (CONV_W - 1)

    def conv_rows(b, r0):
        acc = jnp.broadcast_to(dwb_ref[...], (chunk // SUBLANES, SUBLANES, CONV_CH))
        for j in range(CONV_W):
            s, base = (first + j) % SUBLANES, (first + j) // SUBLANES * SUBLANES
            taps = ext_ref[s, b, r0 + base:r0 + base + chunk, :]
            acc = acc + taps.reshape(chunk // SUBLANES, SUBLANES, CONV_CH) * dww_ref[j]
        acc = acc.reshape(chunk, CONV_CH)
        mu = jnp.mean(acc, axis=-1, keepdims=True)
        xc = acc - mu
        y = xc * lax.rsqrt(jnp.mean(xc * xc, axis=-1, keepdims=True) + EPS)
        y = y * cng_ref[...] + cnb_ref[...]
        act_ref[b * tm + r0:b * tm + r0 + chunk, :] = (y * _sigmoid(y)).astype(BF16)

    for b in range(bb):
        for r0 in range(0, tm, chunk):
            conv_rows(b, r0)
    c = jnp.dot(act_ref[...], wco_ref[...], preferred_element_type=F32) + bco_ref[...]
    a = jnp.dot(a_ref[...].reshape(rows, ATTN_W), wao_ref[...], preferred_element_type=F32)
    m = (sga_ref[...].reshape(rows, D_MODEL).astype(F32) * a
         + sgc_ref[...].reshape(rows, D_MODEL).astype(F32) * c)
    x1 = x_ref[...].reshape(rows, D_MODEL) + jnp.dot(m.astype(BF16), wo_ref[...],
                                                     preferred_element_type=F32)
    x1_ref[...] = x1.reshape(bb, tm, D_MODEL)


def _mix(x, a, u, state, sga, sgc, dww, dwb, cng, cnb, wco, bco, wao, wo, *, layer, bb, tm):
    nb, t, _ = x.shape
    grid = (nb // bb, t // tm)
    row = lambda width: pl.BlockSpec((bb, tm, width), lambda b, i: (b, i, 0))
    halo_blocks = tm // CONV_HALO
    halo = pl.BlockSpec((bb, CONV_HALO, CONV_CH),
                        lambda b, i: (b, jnp.maximum(i * halo_blocks - 1, 0), 0))
    st_spec = pl.BlockSpec((bb, CONV_HALO, CONV_CH), lambda b, i: (b, 0, 0))
    return pl.pallas_call(
        _mix_kernel,
        grid=grid,
        in_specs=[row(D_MODEL), row(ATTN_W), row(CONV_CH), halo, st_spec, row(D_MODEL), row(D_MODEL),
                  _layer_spec((CONV_W, SUBLANES, CONV_CH), layer), _layer_spec((1, CONV_CH), layer),
                  _layer_spec((1, CONV_CH), layer), _layer_spec((1, CONV_CH), layer),
                  _layer_spec((CONV_CH, D_MODEL), layer), _layer_spec((1, D_MODEL), layer),
                  _layer_spec((ATTN_W, D_MODEL), layer), _layer_spec((D_MODEL, D_MODEL), layer)],
        out_specs=row(D_MODEL),
        out_shape=jax.ShapeDtypeStruct((nb, t, D_MODEL), F32),
        scratch_shapes=[pltpu.VMEM((SUBLANES, bb, CONV_HALO + tm, CONV_CH), F32),
                        pltpu.VMEM((bb * tm, CONV_CH), BF16)],
        compiler_params=_compiler_params(),
        name="mix",
    )(x, a, u, u, state, sga, sgc, dww, dwb, cng, cnb, wco, bco, wao, wo)


def _ffn_kernel(x1_ref, st_ref, g2_ref, wup_ref, fdw_ref, fdb_ref, wdn_ref, fg_ref,
                y_ref, nf_ref, prev_ref, act_ref, *, final_norm):
    i = pl.program_id(1)
    bb, tm, _ = x1_ref.shape
    rows = bb * tm

    @pl.when(i == 0)
    def _():
        prev_ref[...] = st_ref[...]

    x1 = x1_ref[...].reshape(rows, D_MODEL)
    h = x1 * lax.rsqrt(jnp.mean(x1 * x1, axis=-1, keepdims=True) + EPS) * g2_ref[...]
    h = h.astype(BF16)

    first_rows = lax.broadcasted_iota(jnp.int32, (bb, FFN_HALO, FFN_CHUNK), 1)

    def conv(c0):
        up = jnp.dot(h, wup_ref[:, c0:c0 + FFN_CHUNK], preferred_element_type=F32)
        up = up.reshape(bb, tm, FFN_CHUNK)
        before = prev_ref[:, :, c0:c0 + FFN_CHUNK]
        tail = up[:, tm - FFN_HALO:tm, :]
        prev_ref[:, :, c0:c0 + FFN_CHUNK] = tail
        nf_ref[:, :, c0:c0 + FFN_CHUNK] = tail
        w = fdw_ref[:, c0:c0 + FFN_CHUNK]
        out = up * w[FFN_CONV_W - 1:FFN_CONV_W] + fdb_ref[:, c0:c0 + FFN_CHUNK]
        for back in range(1, FFN_CONV_W):
            moved = pltpu.roll(up, back, axis=1)
            head = jnp.where(first_rows < back, pltpu.roll(before, back, axis=1), moved[:, 0:FFN_HALO, :])
            moved = jnp.concatenate([head, moved[:, FFN_HALO:, :]], axis=1)
            out = out + moved * w[FFN_CONV_W - 1 - back:FFN_CONV_W - back]
        return out.reshape(rows, FFN_CHUNK)

    for c in range(D_FF // FFN_CHUNK):
        gate = conv(c * FFN_CHUNK)
        val = conv(D_FF + c * FFN_CHUNK)
        act_ref[:, c * FFN_CHUNK:(c + 1) * FFN_CHUNK] = (gate * _sigmoid(gate) * val).astype(BF16)

    y = x1 + jnp.dot(act_ref[...], wdn_ref[...], preferred_element_type=F32)
    if final_norm:
        y = y * lax.rsqrt(jnp.mean(y * y, axis=-1, keepdims=True) + EPS) * fg_ref[...]
    y_ref[...] = y.reshape(bb, tm, D_MODEL)


def _ffn(x1, state, g2, wup, fdw, fdb, wdn, fg, *, layer, bb, tm, final_norm):
    nb, t, _ = x1.shape
    grid = (nb // bb, t // tm)
    row = pl.BlockSpec((bb, tm, D_MODEL), lambda b, i: (b, i, 0))
    st_spec = pl.BlockSpec((bb, FFN_HALO, 2 * D_FF), lambda b, i: (b, 0, 0))
    return pl.pallas_call(
        functools.partial(_ffn_kernel, final_norm=final_norm),
        grid=grid,
        in_specs=[row, st_spec, _layer_spec((1, D_MODEL), layer), _layer_spec((D_MODEL, 2 * D_FF), layer),
                  _layer_spec((FFN_CONV_W, 2 * D_FF), layer), _layer_spec((1, 2 * D_FF), layer),
                  _layer_spec((D_FF, D_MODEL), layer), _const_spec((1, D_MODEL))],
        out_specs=[row, st_spec],
        out_shape=[jax.ShapeDtypeStruct((nb, t, D_MODEL), F32),
                   jax.ShapeDtypeStruct((nb, FFN_HALO, 2 * D_FF), F32)],
        scratch_shapes=[pltpu.VMEM((bb, FFN_HALO, 2 * D_FF), F32),
                        pltpu.VMEM((bb * tm, D_FF), BF16)],
        compiler_params=_compiler_params(),
        name="ffn",
    )(x1, state, g2, wup, fdw, fdb, wdn, fg)


def _cumsum_matrix():
    j = np.arange(2 * KEY_BLOCK)[:, None]
    s = np.arange(2 * KEY_BLOCK)[None, :]
    same = (j // KEY_BLOCK) == (s // KEY_BLOCK)
    return (same & ((j % KEY_BLOCK) > (s % KEY_BLOCK))).astype(np.float32)


def _head_mean_matrix():
    h = np.arange(ATTN_W) // HEAD_DIM
    return (h[:, None] == h[None, :]).astype(np.float32)


def _pad_rows_front(a, rows):
    return jnp.pad(a, ((0, 0), (rows - a.shape[1], 0), (0, 0)))


def kernel(x_prompt, x_sample, cache_sb_k, cache_sb_v, state_conv, state_ffn_conv, norm1_g, w_in, b_in,
           q_norm_g, k_norm_g, w_attn_out, dw_w, dw_b, cn_g, cn_b, w_conv_out, b_conv_out, w_out,
           norm2_g, w_up, ffn_dw_w, ffn_dw_b, w_down, final_g):
    depth = w_in.shape[0]
    bp, seq, _ = x_prompt.shape
    bs, dec, _ = x_sample.shape
    past = cache_sb_k.shape[2]
    assert all(seq % tile == 0 for tile in PROMPT_TILES[1:]) and past % KEY_BLOCK == 0
    assert CONV_W - 1 <= dec <= KEY_BLOCK and dec % (2 * SUBLANES) == 0
    um = jnp.asarray(_cumsum_matrix(), BF16)
    hm = jnp.asarray(_head_mean_matrix(), BF16)

    feature_major = lambda c: jnp.transpose(c, (0, 1, 3, 4, 2)).reshape(depth, bs, ATTN_W, past)
    kt_cache, vt_cache = feature_major(cache_sb_k), feature_major(cache_sb_v)
    conv_zero = jnp.zeros((bp, CONV_HALO, CONV_CH), F32)
    ffn_zero = jnp.zeros((bp, FFN_HALO, 2 * D_FF), F32)

    vec = lambda a: a.reshape(depth, 1, -1)
    per_head = lambda g: vec(jnp.tile(g, (1, N_HEADS)))
    proj_params = (vec(norm1_g), w_in.astype(BF16), vec(b_in), per_head(q_norm_g), per_head(k_norm_g))
    mix_params = (jnp.broadcast_to(dw_w[:, :, None, :], (depth, CONV_W, SUBLANES, CONV_CH)), vec(dw_b), vec(cn_g),
                  vec(cn_b), w_conv_out.astype(BF16), vec(b_conv_out), w_attn_out.astype(BF16), w_out.astype(BF16))
    ffn_params = (vec(norm2_g), w_up.astype(BF16), ffn_dw_w, vec(ffn_dw_b), w_down.astype(BF16))

    def run(l, x, conv_state, ffn_state, tiles, kv_stack=None):
        bb = tiles.batch
        q, k, v, kbt, vbd, u, sga, sgc = _inproj(x, *proj_params, hm, layer=l, bb=bb, tm=tiles.proj,
                                                 kv_stack=kv_stack)
        t = x.shape[1]
        if kv_stack is None:
            a = _attention(q, kbt, vbd, um, tq=tiles.attn, kt_cache=kt_cache, vt_cache=vt_cache, layer=l)
        else:
            a = _attention(q, kbt, vbd, um, tq=tiles.attn)
        x1 = _mix(x, a, u, conv_state, sga, sgc, *mix_params, layer=l, bb=bb, tm=tiles.mix)
        y, nf = _ffn(x1, ffn_state, *ffn_params, final_g.reshape(1, -1), layer=l, bb=bb, tm=tiles.ffn,
                     final_norm=l == depth - 1)
        return y, k, v, u[:, t - (CONV_W - 1):, :], nf[:, FFN_HALO - (FFN_CONV_W - 1):, :]

    xp, xs = x_prompt, x_sample
    kp, vp = jnp.zeros((depth, bp, ATTN_W, seq), F32), jnp.zeros((depth, bp, ATTN_W, seq), F32)
    outs = {name: [] for name in ("ks", "vs", "cp", "cs", "fp", "fs")}
    for l in range(depth):
        xp, kp, vp, c1, f1 = run(l, xp, conv_zero, ffn_zero, PROMPT_TILES, kv_stack=(kp, vp))
        xs, k2, v2, c2, f2 = run(l, xs, _pad_rows_front(state_conv[l], CONV_HALO),
                                 _pad_rows_front(state_ffn_conv[l], FFN_HALO),
                                 Tiles(batch=bs, proj=dec, attn=dec, mix=dec, ffn=dec))
        for name, val in zip(("ks", "vs", "cp", "cs", "fp", "fs"), (k2, v2, c1, c2, f1, f2)):
            outs[name].append(val)

    heads = lambda lst, b, t: jnp.stack(lst).reshape(depth, b, t, N_HEADS, HEAD_DIM)
    heads_t = lambda a: jnp.transpose(a.reshape(depth, bp, N_HEADS, HEAD_DIM, seq), (0, 1, 4, 2, 3))
    return (xp, xs, heads_t(kp), heads_t(vp), heads(outs["ks"], bs, dec), heads(outs["vs"], bs, dec),
            jnp.stack(outs["cp"]), jnp.stack(outs["cs"]), jnp.stack(outs["fp"]), jnp.stack(outs["fs"]))
```
